```python
import jax, jax.numpy as jnp
from jax import lax
import numpy as np

D_MODEL = 1024
BATCH = 32
SEQ = 2048
DEPTH = 2

N_HEADS = 16
HEAD_DIM = D_MODEL // N_HEADS
D_FF = ((8 * D_MODEL // 3 + 127) // 128) * 128
CONV_WIDTH = 3
CHUNK = 64
Q_BLOCK = 128
N_A_LAYERS = DEPTH // 2
N_B_LAYERS = DEPTH - N_A_LAYERS
RMS_EPS = 1e-6
FORGET_BIAS = 3.0

kernel_name = "yoco_stickbreak_fox_convffn"


def rmsnorm(x, g):
    xf = x.astype(jnp.float32)
    y = xf * lax.rsqrt(jnp.mean(xf * xf, axis=-1, keepdims=True) + RMS_EPS)
    return (y * g.astype(jnp.float32)).astype(x.dtype)


def split_heads(t):
    b, s, _ = t.shape
    return t.reshape(b, s, N_HEADS, HEAD_DIM).transpose(0, 2, 1, 3)


def merge_heads(t):
    b, h, s, d = t.shape
    return t.transpose(0, 2, 1, 3).reshape(b, s, h * d)


def stick_breaking_attention(q, k, v):
    b, h, s, d = q.shape
    scale = HEAD_DIM ** -0.5
    n_blocks = s // Q_BLOCK
    strict = jnp.arange(Q_BLOCK)[:, None] > jnp.arange(Q_BLOCK)[None, :]
    outs = []
    for qi in range(n_blocks):
        lo, hi = qi * Q_BLOCK, (qi + 1) * Q_BLOCK
        qb = q[:, :, lo:hi].astype(jnp.float32) * scale
        kd = k[:, :, lo:hi].astype(jnp.float32)
        vd = v[:, :, lo:hi].astype(jnp.float32)
        z = jnp.einsum('bhtd,bhsd->bhts', qb, kd)
        log_keep = jnp.where(strict, jax.nn.log_sigmoid(-z), 0.0)
        later = lax.cumsum(log_keep, axis=3, reverse=True) - log_keep
        attn = jnp.where(strict, jnp.exp(jax.nn.log_sigmoid(z) + later), 0.0)
        out = jnp.einsum('bhts,bhsd->bhtd', attn, vd)
        log_surv = jnp.sum(log_keep, axis=-1)
        if qi > 0:
            kp = k[:, :, :lo].astype(jnp.float32).reshape(b, h, qi, Q_BLOCK, d)
            vp = v[:, :, :lo].astype(jnp.float32).reshape(b, h, qi, Q_BLOCK, d)
            kp = jnp.flip(jnp.moveaxis(kp, 2, 0), axis=0)
            vp = jnp.flip(jnp.moveaxis(vp, 2, 0), axis=0)

            def step(carry, kv, qb=qb):
                acc, surv = carry
                kb, vb = kv
                zb = jnp.einsum('bhtd,bhsd->bhts', qb, kb)
                lk = jax.nn.log_sigmoid(-zb)
                lat = lax.cumsum(lk, axis=3, reverse=True) - lk + surv[..., None]
                wb = jnp.exp(jax.nn.log_sigmoid(zb) + lat)
                acc = acc + jnp.einsum('bhts,bhsd->bhtd', wb, vb)
                return (acc, surv + jnp.sum(lk, axis=-1)), None

            (out, _), _ = lax.scan(step, (out, log_surv), (kp, vp))
        outs.append(out)
    return jnp.concatenate(outs, axis=2).astype(v.dtype)


def forgetting_attention(q, k, v, fcum):
    s = q.shape[2]
    scale = HEAD_DIM ** -0.5
    outs = []
    for qi in range(s // Q_BLOCK):
        lo, hi = qi * Q_BLOCK, (qi + 1) * Q_BLOCK
        qb = q[:, :, lo:hi].astype(jnp.float32) * scale
        kb = k[:, :, :hi].astype(jnp.float32)
        vb = v[:, :, :hi].astype(jnp.float32)
        logits = (jnp.einsum('bhtd,bhsd->bhts', qb, kb)
                  + fcum[:, :, lo:hi, None] - fcum[:, :, None, :hi])
        causal = (lo + jnp.arange(Q_BLOCK))[:, None] >= jnp.arange(hi)[None, :]
        p = jax.nn.softmax(jnp.where(causal, logits, -jnp.inf), axis=-1)
        outs.append(jnp.einsum('bhts,bhsd->bhtd', p, vb))
    return jnp.concatenate(outs, axis=2).astype(v.dtype)


def conv_ffn(x, w_up, conv_w, conv_b, w_down):
    hdn = x @ w_up
    hdn = lax.conv_general_dilated(
        hdn, conv_w[:, None, :], window_strides=(1,),
        padding=[(CONV_WIDTH - 1, 0)],
        dimension_numbers=('NWC', 'WIO', 'NWC'),
        feature_group_count=2 * D_FF) + conv_b
    gate, up = jnp.split(hdn, 2, axis=-1)
    return (jax.nn.gelu(gate, approximate=True) * up) @ w_down


def setup_inputs(seed: int = 0) -> dict:
    key = jax.random.key(seed)
    ks = jax.random.split(key, 20)
    D, F, H = D_MODEL, D_FF, N_HEADS

    def w(k, shape, fan_in):
        return jax.random.normal(k, shape, jnp.float32) * fan_in ** -0.5

    def gain(k, shape):
        return 1.0 + 0.1 * jax.random.normal(k, shape, jnp.float32)

    return {
        "x": jax.random.normal(ks[0], (BATCH, SEQ, D), jnp.float32),
        "sb_pre_g": gain(ks[1], (N_A_LAYERS, D)),
        "sb_w_qkv": w(ks[2], (N_A_LAYERS, D, 3 * D), D),
        "sb_w_o": w(ks[3], (N_A_LAYERS, D, D), D),
        "sb_post_g": gain(ks[4], (N_A_LAYERS, D)),
        "kv_norm_g": gain(ks[5], (D,)),
        "w_kvf": w(ks[6], (D, 2 * D + H), D),
        "b_f": FORGET_BIAS + 0.5 * jax.random.normal(ks[7], (H,), jnp.float32),
        "fox_pre_g": gain(ks[8], (N_B_LAYERS, D)),
        "fox_w_q": w(ks[9], (N_B_LAYERS, D, D), D),
        "fox_w_o": w(ks[10], (N_B_LAYERS, D, D), D),
        "fox_post_g": gain(ks[11], (N_B_LAYERS, D)),
        "ffn_pre_g": gain(ks[12], (DEPTH, D)),
        "w_up": w(ks[13], (DEPTH, D, 2 * F), D),
        "conv_w": w(ks[14], (DEPTH, CONV_WIDTH, 2 * F), CONV_WIDTH),
        "conv_b": 0.02 * jax.random.normal(ks[15], (DEPTH, 2 * F), jnp.float32),
        "w_down": w(ks[16], (DEPTH, F, D), F),
        "ffn_post_g": gain(ks[17], (DEPTH, D)),
    }


def reference(x, sb_pre_g, sb_w_qkv, sb_w_o, sb_post_g, kv_norm_g, w_kvf, b_f,
              fox_pre_g, fox_w_q, fox_w_o, fox_post_g,
              ffn_pre_g, w_up, conv_w, conv_b, w_down, ffn_post_g):
    D = D_MODEL
    h = x
    k_sh = v_sh = fcum_sh = None
    for layer in range(DEPTH):
        if layer < N_A_LAYERS:
            a = rmsnorm(h, sb_pre_g[layer])
            q, k, v = jnp.split(a @ sb_w_qkv[layer], 3, axis=-1)
            o = stick_breaking_attention(split_heads(q), split_heads(k), split_heads(v))
            h = h + rmsnorm(merge_heads(o) @ sb_w_o[layer], sb_post_g[layer])
        else:
            if layer == N_A_LAYERS:
                s_in = rmsnorm(h, kv_norm_g)
                kvf = s_in @ w_kvf
                k_sh = split_heads(kvf[..., :D])
                v_sh = split_heads(kvf[..., D:2 * D])
                log_f = jax.nn.log_sigmoid(kvf[..., 2 * D:].astype(jnp.float32)
                                           + b_f.astype(jnp.float32))
                fcum_sh = jnp.cumsum(jnp.transpose(log_f, (0, 2, 1)), axis=-1)
            j = layer - N_A_LAYERS
            a = rmsnorm(h, fox_pre_g[j])
            q = split_heads(a @ fox_w_q[j])
            o = forgetting_attention(q, k_sh, v_sh, fcum_sh)
            h = h + rmsnorm(merge_heads(o) @ fox_w_o[j], fox_post_g[j])
        f = conv_ffn(rmsnorm(h, ffn_pre_g[layer]), w_up[layer], conv_w[layer],
                     conv_b[layer], w_down[layer])
        h = h + rmsnorm(f, ffn_post_g[layer])
    return h
```

```python
import functools

import jax
import jax.numpy as jnp
from jax import lax
from jax.experimental import pallas as pl
from jax.experimental.pallas import tpu as pltpu

D_MODEL = 1024
N_HEADS = 16
HEAD_DIM = D_MODEL // N_HEADS
D_FF = 2816
RMS_EPS = 1e-6
Q_SCALE = HEAD_DIM ** -0.5

LANES = 128
SUBLANES = 8
ROW_TILE = 512
Q_TILE = 256
K_TILE = 256
FF_CHUNK = 256
NEG_BIG = -1e30
VMEM_LIMIT = 56 * 1024 * 1024

f32 = jnp.float32
bf16 = jnp.bfloat16


def _const_spec(shape):
    return pl.BlockSpec(shape, lambda *_: (0,) * len(shape),
                        pipeline_mode=pl.Buffered(1))


def _params(n_axes):
    return pltpu.CompilerParams(
        dimension_semantics=("arbitrary",) * n_axes,
        vmem_limit_bytes=VMEM_LIMIT)


def _inv_rms(x):
    return lax.rsqrt(jnp.mean(x * x, axis=-1, keepdims=True) + RMS_EPS)


def _softplus(z):
    return jnp.maximum(z, 0.0) + jnp.log(1.0 + jnp.exp(-jnp.abs(z)))


def _dot(a, b):
    return jnp.dot(a, b, preferred_element_type=f32)


def _dot_nt(a, b):
    return lax.dot_general(a, b, (((1,), (1,)), ((), ())),
                           preferred_element_type=f32)


def _qkv_kernel(x_ref, g_ref, w_ref, q_ref, k_ref, v_ref):
    x = x_ref[...]
    a = (x * _inv_rms(x) * g_ref[...]).astype(bf16)
    d = D_MODEL
    q_ref[...] = (_dot(a, w_ref[:, 0:d]) * Q_SCALE).astype(bf16)
    k_ref[...] = _dot(a, w_ref[:, d:2 * d]).astype(bf16)
    v_ref[...] = _dot(a, w_ref[:, 2 * d:3 * d]).astype(bf16)


def _qkv_proj(h, g, w):
    n, d = h.shape
    row = pl.BlockSpec((ROW_TILE, d), lambda i: (i, 0))
    out = jax.ShapeDtypeStruct((n, d), bf16)
    return pl.pallas_call(
        _qkv_kernel,
        grid=(n // ROW_TILE,),
        in_specs=[row, _const_spec((1, d)), _const_spec((d, 3 * d))],
        out_specs=[row, row, row],
        out_shape=[out, out, out],
        compiler_params=_params(1),
        name="qkv_proj",
    )(h, g, w)


def _kvq_kernel(x_ref, gkv_ref, gq_ref, wkv_ref, wf_ref, bf_ref, wq_ref,
                k_ref, v_ref, q_ref, fc_ref, carry_ref, *, tiles_per_seq):
    d = D_MODEL
    x = x_ref[...]
    xn = x * _inv_rms(x)
    s_in = (xn * gkv_ref[...]).astype(bf16)
    a = (xn * gq_ref[...]).astype(bf16)
    k_ref[...] = _dot(s_in, wkv_ref[:, 0:d]).astype(bf16)
    v_ref[...] = _dot(s_in, wkv_ref[:, d:2 * d]).astype(bf16)
    q_ref[...] = (_dot(a, wq_ref[...]) * Q_SCALE).astype(bf16)

    fl = _dot(s_in, wf_ref[...]) + bf_ref[...]
    log_f = jnp.minimum(fl, 0.0) - jnp.log(1.0 + jnp.exp(-jnp.abs(fl)))
    hi = log_f.astype(bf16)
    lo = (log_f - hi.astype(f32)).astype(bf16)
    r = lax.broadcasted_iota(jnp.int32, (ROW_TILE, ROW_TILE), 0)
    c = lax.broadcasted_iota(jnp.int32, (ROW_TILE, ROW_TILE), 1)
    tri = (r >= c).astype(bf16)
    csum = _dot(tri, hi) + _dot(tri, lo)

    @pl.when(pl.program_id(0) % tiles_per_seq == 0)
    def _():
        carry_ref[...] = jnp.zeros_like(carry_ref)

    fcum = csum + carry_ref[0:1, :]
    fc_ref[...] = fcum
    carry_ref[...] = jnp.broadcast_to(fcum[ROW_TILE - 1:ROW_TILE, :],
                                      carry_ref.shape)


def _kvq_proj(h, gkv, gq, wkv, wf, bfp, wq, *, seq):
    n, d = h.shape
    row = pl.BlockSpec((ROW_TILE, d), lambda i: (i, 0))
    out = jax.ShapeDtypeStruct((n, d), bf16)
    return pl.pallas_call(
        functools.partial(_kvq_kernel, tiles_per_seq=seq // ROW_TILE),
        grid=(n // ROW_TILE,),
        in_specs=[row, _const_spec((1, d)), _const_spec((1, d)),
                  _const_spec((d, 2 * d)), _const_spec((d, LANES)),
                  _const_spec((1, LANES)), _const_spec((d, d))],
        out_specs=[row, row, row,
                   pl.BlockSpec((ROW_TILE, LANES), lambda i: (i, 0))],
        out_shape=[out, out, out, jax.ShapeDtypeStruct((n, LANES), f32)],
        scratch_shapes=[pltpu.VMEM((SUBLANES, LANES), f32)],
        compiler_params=_params(1),
        name="kvq_proj",
    )(h, gkv, gq, wkv, wf, bfp, wq)


def _split_heads(q2):
    lane = lax.broadcasted_iota(jnp.int32, q2.shape, 1)
    qf = q2.astype(f32)
    return (jnp.where(lane < HEAD_DIM, qf, 0.0).astype(bf16),
            jnp.where(lane < HEAD_DIM, 0.0, qf).astype(bf16))


def _sb_attn_kernel(q_ref, k_ref, v_ref, u_ref, o_ref, acc_ref, *, seq):
    lane = lax.broadcasted_iota(jnp.int32, (Q_TILE, LANES), 1)
    lo_half = lane < HEAD_DIM

    def block(qh, k0, survs, masked):
        kt = k_ref[0, pl.ds(k0, K_TILE), :]
        vt = v_ref[0, pl.ds(k0, K_TILE), :]
        if masked:
            r = lax.broadcasted_iota(jnp.int32, (Q_TILE, K_TILE), 0)
            c = lax.broadcasted_iota(jnp.int32, (Q_TILE, K_TILE), 1)
            strict = c < r
        pvs, new_survs = [], []
        for h in range(2):
            z = _dot_nt(qh[h], kt)
            lk = -_softplus(z)
            if masked:
                lk = jnp.where(strict, lk, 0.0)
            later = _dot(lk.astype(bf16), u_ref[...])
            w = jnp.exp(z + lk + later + survs[h])
            if masked:
                w = jnp.where(strict, w, 0.0)
            pvs.append(_dot(w.astype(bf16), vt))
            new_survs.append(survs[h] + jnp.sum(lk, axis=-1, keepdims=True))
        return jnp.where(lo_half, pvs[0], pvs[1]), tuple(new_survs)

    def q_tile(qi, carry):
        q0 = pl.multiple_of(qi * Q_TILE, Q_TILE)
        qh = _split_heads(q_ref[0, pl.ds(q0, Q_TILE), :])
        zero = jnp.zeros((Q_TILE, 1), f32)
        pv, survs = block(qh, q0, (zero, zero), True)
        acc_ref[...] = pv

        def body(i, survs):
            k0 = pl.multiple_of((qi - 1 - i) * K_TILE, K_TILE)
            pv, survs = block(qh, k0, survs, False)
            acc_ref[...] += pv
            return survs

        lax.fori_loop(0, qi, body, survs)
        o_ref[0, pl.ds(q0, Q_TILE), :] = acc_ref[...].astype(o_ref.dtype)
        return carry

    lax.fori_loop(0, seq // Q_TILE, q_tile, 0)


def _sb_attention(q, k, v, upper):
    b, s, d = q.shape
    blk = pl.BlockSpec((1, s, LANES), lambda bi, hp: (bi, 0, hp))
    return pl.pallas_call(
        functools.partial(_sb_attn_kernel, seq=s),
        grid=(b, d // LANES),
        in_specs=[blk, blk, blk, _const_spec((K_TILE, K_TILE))],
        out_specs=blk,
        out_shape=jax.ShapeDtypeStruct((b, s, d), bf16),
        scratch_shapes=[pltpu.VMEM((Q_TILE, LANES), f32)],
        compiler_params=_params(2),
        name="sb_attention",
    )(q, k, v, upper)


def _fox_attn_kernel(q_ref, k_ref, v_ref, fc_ref, fr_ref, o_ref,
                     acc_ref, *, seq):
    hp = pl.program_id(1)
    lane = lax.broadcasted_iota(jnp.int32, (Q_TILE, LANES), 1)
    lo_half = lane < HEAD_DIM

    def block(qh, fq, kj, state, masked):
        k0 = pl.multiple_of(kj * K_TILE, K_TILE)
        kt = k_ref[0, pl.ds(k0, K_TILE), :]
        vt = v_ref[0, pl.ds(k0, K_TILE), :]
        if masked:
            r = lax.broadcasted_iota(jnp.int32, (Q_TILE, K_TILE), 0)
            c = lax.broadcasted_iota(jnp.int32, (Q_TILE, K_TILE), 1)
            causal = c <= r
        pvs, alphas, new_state = [], [], []
        for h in range(2):
            m, l = state[h]
            fk = fr_ref[0, 2 * hp + h, pl.ds(kj, 1), :]
            s = _dot_nt(qh[h], kt) + fq[h] - fk
            if masked:
                s = jnp.where(causal, s, NEG_BIG)
            m_new = jnp.maximum(m, jnp.max(s, axis=-1, keepdims=True))
            alpha = jnp.exp(m - m_new)
            p = jnp.exp(s - m_new)
            l_new = alpha * l + jnp.sum(p, axis=-1, keepdims=True)
            pvs.append(_dot(p.astype(bf16), vt))
            alphas.append(alpha)
            new_state.append((m_new, l_new))
        pv = jnp.where(lo_half, pvs[0], pvs[1])
        alpha = jnp.where(lo_half, alphas[0], alphas[1])
        return pv, alpha, tuple(new_state)

    def q_tile(qi, carry):
        q0 = pl.multiple_of(qi * Q_TILE, Q_TILE)
        qh = _split_heads(q_ref[0, pl.ds(q0, Q_TILE), :])
        fc = fc_ref[0, pl.ds(q0, Q_TILE), :]
        fq = tuple(
            jnp.sum(jnp.where(lane == 2 * hp + h, fc, 0.0), axis=-1,
                    keepdims=True) for h in range(2))
        init = (jnp.full((Q_TILE, 1), NEG_BIG, f32), jnp.zeros((Q_TILE, 1), f32))
        pv, _, state = block(qh, fq, qi, (init, init), True)
        acc_ref[...] = pv

        def body(kj, state):
            pv, alpha, state = block(qh, fq, kj, state, False)
            acc_ref[...] = alpha * acc_ref[...] + pv
            return state

        state = lax.fori_loop(0, qi, body, state)
        l = jnp.where(lo_half, state[0][1], state[1][1])
        o_ref[0, pl.ds(q0, Q_TILE), :] = (acc_ref[...] / l).astype(o_ref.dtype)
        return carry

    lax.fori_loop(0, seq // Q_TILE, q_tile, 0)


def _fox_attention(q, k, v, fcum_c, fcum_r):
    b, s, d = q.shape
    blk = pl.BlockSpec((1, s, LANES), lambda bi, hp: (bi, 0, hp))
    nk = s // K_TILE
    return pl.pallas_call(
        functools.partial(_fox_attn_kernel, seq=s),
        grid=(b, d // LANES),
        in_specs=[blk, blk, blk,
                  pl.BlockSpec((1, s, LANES), lambda bi, hp: (bi, 0, 0)),
                  pl.BlockSpec((1, N_HEADS, nk, K_TILE),
                               lambda bi, hp: (bi, 0, 0, 0))],
        out_specs=blk,
        out_shape=jax.ShapeDtypeStruct((b, s, d), bf16),
        scratch_shapes=[pltpu.VMEM((Q_TILE, LANES), f32)],
        compiler_params=_params(2),
        name="fox_attention",
    )(q, k, v, fcum_c, fcum_r)


def _oproj_kernel(o_ref, h_ref, w_ref, g_ref, out_ref):
    y = _dot(o_ref[...], w_ref[...])
    out_ref[...] = h_ref[...] + y * _inv_rms(y) * g_ref[...]


def _out_proj(o, h, w, g):
    n, d = h.shape
    row = pl.BlockSpec((ROW_TILE, d), lambda i: (i, 0))
    return pl.pallas_call(
        _oproj_kernel,
        grid=(n // ROW_TILE,),
        in_specs=[row, row, _const_spec((d, d)), _const_spec((1, d))],
        out_specs=row,
        out_shape=jax.ShapeDtypeStruct((n, d), f32),
        compiler_params=_params(1),
        name="out_proj",
    )(o, h, w, g)


def _gelu_tanh(x):
    c = 0.7978845608028654
    return 0.5 * x * (1.0 + jnp.tanh(c * (x + 0.044715 * (x * x * x))))


def _ffn_kernel(h_ref, g1_ref, wup_ref, cw_ref, cb_ref, wdn_ref, g2_ref,
                out_ref, act_ref, halo_ref, *, tiles_per_seq):
    x = h_ref[...]
    a = (x * _inv_rms(x) * g1_ref[...]).astype(bf16)

    @pl.when(pl.program_id(0) % tiles_per_seq == 0)
    def _():
        halo_ref[...] = jnp.zeros_like(halo_ref)

    def conv(cols):
        hdn = _dot(a, wup_ref[:, cols])
        xx = jnp.concatenate([halo_ref[:, cols], hdn], axis=0)
        halo_ref[:, cols] = hdn[ROW_TILE - SUBLANES:, :]
        prev1 = pltpu.roll(xx, 1, 0)[SUBLANES:, :]
        prev2 = pltpu.roll(xx, 2, 0)[SUBLANES:, :]
        return (cw_ref[2:3, cols] * hdn + cw_ref[1:2, cols] * prev1
                + cw_ref[0:1, cols] * prev2 + cb_ref[:, cols])

    for c in range(D_FF // FF_CHUNK):
        gate = conv(slice(c * FF_CHUNK, (c + 1) * FF_CHUNK))
        up = conv(slice(D_FF + c * FF_CHUNK, D_FF + (c + 1) * FF_CHUNK))
        act_ref[:, c * FF_CHUNK:(c + 1) * FF_CHUNK] = (
            _gelu_tanh(gate) * up).astype(bf16)

    y = _dot(act_ref[...], wdn_ref[...])
    out_ref[...] = x + y * _inv_rms(y) * g2_ref[...]


def _conv_ffn(h, g1, wup, cw, cb, wdn, g2, *, seq):
    n, d = h.shape
    row = pl.BlockSpec((ROW_TILE, d), lambda i: (i, 0))
    return pl.pallas_call(
        functools.partial(_ffn_kernel, tiles_per_seq=seq // ROW_TILE),
        grid=(n // ROW_TILE,),
        in_specs=[row, _const_spec((1, d)), _const_spec((d, 2 * D_FF)),
                  _const_spec((3, 2 * D_FF)), _const_spec((1, 2 * D_FF)),
                  _const_spec((D_FF, d)), _const_spec((1, d))],
        out_specs=row,
        out_shape=jax.ShapeDtypeStruct((n, d), f32),
        scratch_shapes=[pltpu.VMEM((ROW_TILE, D_FF), bf16),
                        pltpu.VMEM((SUBLANES, 2 * D_FF), f32)],
        compiler_params=_params(1),
        name="conv_ffn",
    )(h, g1, wup, cw, cb, wdn, g2)


def kernel(x, sb_pre_g, sb_w_qkv, sb_w_o, sb_post_g, kv_norm_g, w_kvf, b_f,
           fox_pre_g, fox_w_q, fox_w_o, fox_post_g,
           ffn_pre_g, w_up, conv_w, conv_b, w_down, ffn_post_g):
    b, s, d = x.shape
    n = b * s
    assert d == D_MODEL and s % ROW_TILE == 0 and s % Q_TILE == 0
    assert sb_pre_g.shape[0] == 1 and fox_pre_g.shape[0] == 1

    def row(g):
        return g.reshape(1, -1).astype(f32)

    def ffn(h, layer):
        return _conv_ffn(h, row(ffn_pre_g[layer]), w_up[layer].astype(bf16),
                         conv_w[layer].astype(f32), row(conv_b[layer]),
                         w_down[layer].astype(bf16), row(ffn_post_g[layer]),
                         seq=s)

    h = x.reshape(n, d)

    q, k, v = _qkv_proj(h, row(sb_pre_g[0]), sb_w_qkv[0].astype(bf16))
    ri = lax.broadcasted_iota(jnp.int32, (K_TILE, K_TILE), 0)
    ci = lax.broadcasted_iota(jnp.int32, (K_TILE, K_TILE), 1)
    upper = (ri > ci).astype(bf16)
    o = _sb_attention(q.reshape(b, s, d), k.reshape(b, s, d),
                      v.reshape(b, s, d), upper)
    h = _out_proj(o.reshape(n, d), h, sb_w_o[0].astype(bf16), row(sb_post_g[0]))
    h = ffn(h, 0)

    w_f = jnp.zeros((d, LANES), bf16).at[:, :N_HEADS].set(
        w_kvf[:, 2 * d:].astype(bf16))
    b_fp = jnp.zeros((1, LANES), f32).at[0, :N_HEADS].set(b_f.astype(f32))
    k, v, q, fcum = _kvq_proj(h, row(kv_norm_g), row(fox_pre_g[0]),
                              w_kvf[:, :2 * d].astype(bf16), w_f, b_fp,
                              fox_w_q[0].astype(bf16), seq=s)
    fcum_c = fcum.reshape(b, s, LANES)
    fcum_r = jnp.transpose(fcum_c[:, :, :N_HEADS], (0, 2, 1)).reshape(
        b, N_HEADS, s // K_TILE, K_TILE)
    o = _fox_attention(q.reshape(b, s, d), k.reshape(b, s, d),
                       v.reshape(b, s, d), fcum_c, fcum_r)
    h = _out_proj(o.reshape(n, d), h, fox_w_o[0].astype(bf16),
                  row(fox_post_g[0]))
    h = ffn(h, 1)
    return h.reshape(b, s, d)
```

```python
import functools
import math

import jax
import jax.numpy as jnp
from jax import lax
from jax.experimental import pallas as pl
from jax.experimental.pallas import tpu as pltpu

D_MODEL = 1024
N_HEADS = 16
HEAD_DIM = D_MODEL // N_HEADS
D_FF = 2816
RMS_EPS = 1e-6
LOG2E = math.log2(math.e)
Q_SCALE = HEAD_DIM ** -0.5 * LOG2E

LANES = 128
SUBLANES = 8
ROW_TILE = 512
Q_SUPER = 512
K_TILE = 256
FF_CHUNK = 256
NEG_BIG = -1e30
VMEM_LIMIT = 56 * 1024 * 1024

f32 = jnp.float32
bf16 = jnp.bfloat16


def _const_spec(shape):
    return pl.BlockSpec(shape, lambda *_: (0,) * len(shape),
                        pipeline_mode=pl.Buffered(1))


def _params(n_axes):
    return pltpu.CompilerParams(
        dimension_semantics=("arbitrary",) * n_axes,
        vmem_limit_bytes=VMEM_LIMIT)


def _inv_rms(x):
    return lax.rsqrt(jnp.mean(x * x, axis=-1, keepdims=True) + RMS_EPS)


def _dot(a, b):
    return jnp.dot(a, b, preferred_element_type=f32)


def _dot_nt(a, b):
    return lax.dot_general(a, b, (((1,), (1,)), ((), ())),
                           preferred_element_type=f32)


def _qkv_kernel(x_ref, g_ref, w_ref, q_ref, k_ref, v_ref):
    x = x_ref[...]
    a = (x * _inv_rms(x) * g_ref[...]).astype(bf16)
    d = D_MODEL
    q_ref[...] = (_dot(a, w_ref[:, 0:d]) * Q_SCALE).astype(bf16)
    k_ref[...] = _dot(a, w_ref[:, d:2 * d]).astype(bf16)
    v_ref[...] = _dot(a, w_ref[:, 2 * d:3 * d]).astype(bf16)


def _qkv_proj(h, g, w):
    n, d = h.shape
    row = pl.BlockSpec((ROW_TILE, d), lambda i: (i, 0))
    out = jax.ShapeDtypeStruct((n, d), bf16)
    return pl.pallas_call(
        _qkv_kernel,
        grid=(n // ROW_TILE,),
        in_specs=[row, _const_spec((1, d)), _const_spec((d, 3 * d))],
        out_specs=[row, row, row],
        out_shape=[out, out, out],
        compiler_params=_params(1),
        name="qkv_proj",
    )(h, g, w)


def _kvq_kernel(x_ref, gkv_ref, gq_ref, wkv_ref, wf_ref, bf_ref, wq_ref,
                k_ref, v_ref, q_ref, fc_ref, carry_ref, *, tiles_per_seq):
    d = D_MODEL
    x = x_ref[...]
    xn = x * _inv_rms(x)
    s_in = (xn * gkv_ref[...]).astype(bf16)
    a = (xn * gq_ref[...]).astype(bf16)
    k_ref[...] = _dot(s_in, wkv_ref[:, 0:d]).astype(bf16)
    v_ref[...] = _dot(s_in, wkv_ref[:, d:2 * d]).astype(bf16)
    q_ref[...] = (_dot(a, wq_ref[...]) * Q_SCALE).astype(bf16)

    fl = _dot(s_in, wf_ref[...]) + bf_ref[...]
    log_f = (jnp.minimum(fl, 0.0)
             - jnp.log(1.0 + jnp.exp(-jnp.abs(fl)))) * LOG2E
    hi = log_f.astype(bf16)
    lo = (log_f - hi.astype(f32)).astype(bf16)
    r = lax.broadcasted_iota(jnp.int32, (ROW_TILE, ROW_TILE), 0)
    c = lax.broadcasted_iota(jnp.int32, (ROW_TILE, ROW_TILE), 1)
    tri = (r >= c).astype(bf16)
    csum = _dot(tri, hi) + _dot(tri, lo)

    @pl.when(pl.program_id(0) % tiles_per_seq == 0)
    def _():
        carry_ref[...] = jnp.zeros_like(carry_ref)

    fcum = csum + carry_ref[0:1, :]
    fc_ref[...] = fcum
    carry_ref[...] = jnp.broadcast_to(fcum[ROW_TILE - 1:ROW_TILE, :],
                                      carry_ref.shape)


def _kvq_proj(h, gkv, gq, wkv, wf, bfp, wq, *, seq):
    n, d = h.shape
    row = pl.BlockSpec((ROW_TILE, d), lambda i: (i, 0))
    out = jax.ShapeDtypeStruct((n, d), bf16)
    return pl.pallas_call(
        functools.partial(_kvq_kernel, tiles_per_seq=seq // ROW_TILE),
        grid=(n // ROW_TILE,),
        in_specs=[row, _const_spec((1, d)), _const_spec((1, d)),
                  _const_spec((d, 2 * d)), _const_spec((d, LANES)),
                  _const_spec((1, LANES)), _const_spec((d, d))],
        out_specs=[row, row, row,
                   pl.BlockSpec((ROW_TILE, LANES), lambda i: (i, 0))],
        out_shape=[out, out, out, jax.ShapeDtypeStruct((n, LANES), f32)],
        scratch_shapes=[pltpu.VMEM((SUBLANES, LANES), f32)],
        compiler_params=_params(1),
        name="kvq_proj",
    )(h, gkv, gq, wkv, wf, bfp, wq)


def _split_heads(q2):
    lane = lax.broadcasted_iota(jnp.int32, q2.shape, 1)
    qf = q2.astype(f32)
    return (jnp.where(lane < HEAD_DIM, qf, 0.0).astype(bf16),
            jnp.where(lane < HEAD_DIM, 0.0, qf).astype(bf16))


def _neg_abs(x):
    bits = lax.bitcast_convert_type(x, jnp.uint32) | jnp.uint32(0x80000000)
    return lax.bitcast_convert_type(bits, f32)


def _block_mask(q0, k0, strict):
    r = lax.broadcasted_iota(jnp.int32, (Q_SUPER, K_TILE), 0) + (q0 - k0)
    c = lax.broadcasted_iota(jnp.int32, (Q_SUPER, K_TILE), 1)
    return c < r if strict else c <= r


def _sb_attn_kernel(q_ref, k_ref, v_ref, nu_ref, o_ref, acc_ref, *, seq):
    lane = lax.broadcasted_iota(jnp.int32, (Q_SUPER, LANES), 1)
    lo_half = lane < HEAD_DIM
    n_diag = Q_SUPER // K_TILE

    def block(qh, q0, k0, survs, masked):
        kt = k_ref[0, pl.ds(k0, K_TILE), :]
        vt = v_ref[0, pl.ds(k0, K_TILE), :]
        if masked:
            mask = _block_mask(q0, k0, strict=True)
        pvs, new_survs = [], []
        for h in range(2):
            z = _dot_nt(qh[h], kt)
            sp = (jnp.maximum(z, 0.0)
                  + jnp.log(1.0 + jnp.exp2(_neg_abs(z))) * LOG2E)
            if masked:
                sp = jnp.where(mask, sp, 0.0)
            later = _dot(sp.astype(bf16), nu_ref[...])
            logw = (z - sp) + later + survs[h]
            if masked:
                logw = jnp.where(mask, logw, NEG_BIG)
            pvs.append(_dot(jnp.exp2(logw).astype(bf16), vt))
            new_survs.append(survs[h] - jnp.sum(sp, axis=-1, keepdims=True))
        return jnp.where(lo_half, pvs[0], pvs[1]), tuple(new_survs)

    def super_tile(qs, carry):
        q0 = pl.multiple_of(qs * Q_SUPER, Q_SUPER)
        qh = _split_heads(q_ref[0, pl.ds(q0, Q_SUPER), :])
        acc_ref[...] = jnp.zeros_like(acc_ref)
        zero = jnp.zeros((Q_SUPER, 1), f32)

        def step(i, survs, masked):
            k0 = pl.multiple_of(q0 + (n_diag - 1 - i) * K_TILE, K_TILE)
            pv, survs = block(qh, q0, k0, survs, masked)
            acc_ref[...] += pv
            return survs

        survs = lax.fori_loop(0, n_diag,
                              functools.partial(step, masked=True), (zero, zero))
        lax.fori_loop(n_diag, n_diag + qs * n_diag,
                      functools.partial(step, masked=False), survs)
        o_ref[0, pl.ds(q0, Q_SUPER), :] = acc_ref[...].astype(o_ref.dtype)
        return carry

    lax.fori_loop(0, seq // Q_SUPER, super_tile, 0)


def _sb_attention(q, k, v, neg_upper):
    b, s, d = q.shape
    blk = pl.BlockSpec((1, s, LANES), lambda bi, hp: (bi, 0, hp))
    return pl.pallas_call(
        functools.partial(_sb_attn_kernel, seq=s),
        grid=(b, d // LANES),
        in_specs=[blk, blk, blk, _const_spec((K_TILE, K_TILE))],
        out_specs=blk,
        out_shape=jax.ShapeDtypeStruct((b, s, d), bf16),
        scratch_shapes=[pltpu.VMEM((Q_SUPER, LANES), f32)],
        compiler_params=_params(2),
        name="sb_attention",
    )(q, k, v, neg_upper)


def _fox_attn_kernel(q_ref, k_ref, v_ref, fc_ref, fr_ref, o_ref,
                     num_ref, den_ref, *, seq):
    hp = pl.program_id(1)
    lane = lax.broadcasted_iota(jnp.int32, (Q_SUPER, LANES), 1)
    lo_half = lane < HEAD_DIM
    klane = lax.broadcasted_iota(jnp.int32, (K_TILE, LANES), 1)
    own_half = (klane < HEAD_DIM, klane >= HEAD_DIM)
    n_diag = Q_SUPER // K_TILE

    def block(qh, fq, q0, kj, ms, masked):
        k0 = pl.multiple_of(kj * K_TILE, K_TILE)
        kt = k_ref[0, pl.ds(k0, K_TILE), :]
        vt = v_ref[0, pl.ds(k0, K_TILE), :].astype(f32)
        if masked:
            mask = _block_mask(q0, k0, strict=False)
        pvs, alphas, new_ms = [], [], []
        for h in range(2):
            fk = fr_ref[0, 2 * hp + h, pl.ds(kj, 1), :]
            s = _dot_nt(qh[h], kt) + fq[h] - fk
            if masked:
                s = jnp.where(mask, s, NEG_BIG)
            m_new = jnp.maximum(ms[h], jnp.max(s, axis=-1, keepdims=True))
            p = jnp.exp2(s - m_new)
            vh = jnp.where(own_half[h], vt, 1.0).astype(bf16)
            pvs.append(_dot(p.astype(bf16), vh))
            alphas.append(jnp.exp2(ms[h] - m_new))
            new_ms.append(m_new)
        return pvs, alphas, tuple(new_ms)

    def super_tile(qs, carry):
        q0 = pl.multiple_of(qs * Q_SUPER, Q_SUPER)
        qh = _split_heads(q_ref[0, pl.ds(q0, Q_SUPER), :])
        fc = fc_ref[0, pl.ds(q0, Q_SUPER), :]
        fq = tuple(
            jnp.sum(jnp.where(lane == 2 * hp + h, fc, 0.0), axis=-1,
                    keepdims=True) for h in range(2))
        num_ref[...] = jnp.zeros_like(num_ref)
        den_ref[...] = jnp.zeros_like(den_ref)
        m0 = jnp.full((Q_SUPER, 1), NEG_BIG, f32)

        def step(i, ms, masked):
            kj = qs * n_diag + i if masked else i
            pvs, alphas, ms = block(qh, fq, q0, kj, ms, masked)
            num_ref[...] = (
                jnp.where(lo_half, alphas[0], alphas[1]) * num_ref[...]
                + jnp.where(lo_half, pvs[0], pvs[1]))
            den_ref[...] = (
                jnp.where(lo_half, alphas[1], alphas[0]) * den_ref[...]
                + jnp.where(lo_half, pvs[1], pvs[0]))
            return ms

        ms = lax.fori_loop(0, n_diag, functools.partial(step, masked=True),
                           (m0, m0))
        lax.fori_loop(0, qs * n_diag, functools.partial(step, masked=False), ms)
        den = pltpu.roll(den_ref[...], HEAD_DIM, 1)
        o_ref[0, pl.ds(q0, Q_SUPER), :] = (num_ref[...] / den).astype(o_ref.dtype)
        return carry

    lax.fori_loop(0, seq // Q_SUPER, super_tile, 0)


def _fox_attention(q, k, v, fcum_c, fcum_r):
    b, s, d = q.shape
    blk = pl.BlockSpec((1, s, LANES), lambda bi, hp: (bi, 0, hp))
    nk = s // K_TILE
    return pl.pallas_call(
        functools.partial(_fox_attn_kernel, seq=s),
        grid=(b, d // LANES),
        in_specs=[blk, blk, blk,
                  pl.BlockSpec((1, s, LANES), lambda bi, hp: (bi, 0, 0)),
                  pl.BlockSpec((1, N_HEADS, nk, K_TILE),
                               lambda bi, hp: (bi, 0, 0, 0))],
        out_specs=blk,
        out_shape=jax.ShapeDtypeStruct((b, s, d), bf16),
        scratch_shapes=[pltpu.VMEM((Q_SUPER, LANES), f32),
                        pltpu.VMEM((Q_SUPER, LANES), f32)],
        compiler_params=_params(2),
        name="fox_attention",
    )(q, k, v, fcum_c, fcum_r)


def _oproj_kernel(o_ref, h_ref, w_ref, g_ref, out_ref):
    y = _dot(o_ref[...], w_ref[...])
    out_ref[...] = h_ref[...] + y * _inv_rms(y) * g_ref[...]


def _out_proj(o, h, w, g):
    n, d = h.shape
    row = pl.BlockSpec((ROW_TILE, d), lambda i: (i, 0))
    return pl.pallas_call(
        _oproj_kernel,
        grid=(n // ROW_TILE,),
        in_specs=[row, row, _const_spec((d, d)), _const_spec((1, d))],
        out_specs=row,
        out_shape=jax.ShapeDtypeStruct((n, d), f32),
        compiler_params=_params(1),
        name="out_proj",
    )(o, h, w, g)


def _gelu_tanh(x):
    c = 0.7978845608028654
    return 0.5 * x * (1.0 + jnp.tanh(c * (x + 0.044715 * (x * x * x))))


def _ffn_kernel(h_ref, g1_ref, wup_ref, cw_ref, cb_ref, wdn_ref, g2_ref,
                out_ref, act_ref, halo_ref, *, tiles_per_seq):
    x = h_ref[...]
    a = (x * _inv_rms(x) * g1_ref[...]).astype(bf16)

    @pl.when(pl.program_id(0) % tiles_per_seq == 0)
    def _():
        halo_ref[...] = jnp.zeros_like(halo_ref)

    def conv(cols):
        hdn = _dot(a, wup_ref[:, cols])
        xx = jnp.concatenate([halo_ref[:, cols], hdn], axis=0)
        halo_ref[:, cols] = hdn[ROW_TILE - SUBLANES:, :]
        prev1 = pltpu.roll(xx, 1, 0)[SUBLANES:, :]
        prev2 = pltpu.roll(xx, 2, 0)[SUBLANES:, :]
        return (cw_ref[2:3, cols] * hdn + cw_ref[1:2, cols] * prev1
                + cw_ref[0:1, cols] * prev2 + cb_ref[:, cols])

    for c in range(D_FF // FF_CHUNK):
        gate = conv(slice(c * FF_CHUNK, (c + 1) * FF_CHUNK))
        up = conv(slice(D_FF + c * FF_CHUNK, D_FF + (c + 1) * FF_CHUNK))
        act_ref[:, c * FF_CHUNK:(c + 1) * FF_CHUNK] = (
            _gelu_tanh(gate) * up).astype(bf16)

    y = _dot(act_ref[...], wdn_ref[...])
    out_ref[...] = x + y * _inv_rms(y) * g2_ref[...]


def _conv_ffn(h, g1, wup, cw, cb, wdn, g2, *, seq):
    n, d = h.shape
    row = pl.BlockSpec((ROW_TILE, d), lambda i: (i, 0))
    return pl.pallas_call(
        functools.partial(_ffn_kernel, tiles_per_seq=seq // ROW_TILE),
        grid=(n // ROW_TILE,),
        in_specs=[row, _const_spec((1, d)), _const_spec((d, 2 * D_FF)),
                  _const_spec((3, 2 * D_FF)), _const_spec((1, 2 * D_FF)),
                  _const_spec((D_FF, d)), _const_spec((1, d))],
        out_specs=row,
        out_shape=jax.ShapeDtypeStruct((n, d), f32),
        scratch_shapes=[pltpu.VMEM((ROW_TILE, D_FF), bf16),
                        pltpu.VMEM((SUBLANES, 2 * D_FF), f32)],
        compiler_params=_params(1),
        name="conv_ffn",
    )(h, g1, wup, cw, cb, wdn, g2)


def kernel(x, sb_pre_g, sb_w_qkv, sb_w_o, sb_post_g, kv_norm_g, w_kvf, b_f,
           fox_pre_g, fox_w_q, fox_w_o, fox_post_g,
           ffn_pre_g, w_up, conv_w, conv_b, w_down, ffn_post_g):
    b, s, d = x.shape
    n = b * s
    assert d == D_MODEL and s % ROW_TILE == 0 and s % Q_SUPER == 0
    assert sb_pre_g.shape[0] == 1 and fox_pre_g.shape[0] == 1

    def row(g):
        return g.reshape(1, -1).astype(f32)

    def ffn(h, layer):
        return _conv_ffn(h, row(ffn_pre_g[layer]), w_up[layer].astype(bf16),
                         conv_w[layer].astype(f32), row(conv_b[layer]),
                         w_down[layer].astype(bf16), row(ffn_post_g[layer]),
                         seq=s)

    h = x.reshape(n, d)

    q, k, v = _qkv_proj(h, row(sb_pre_g[0]), sb_w_qkv[0].astype(bf16))
    ri = lax.broadcasted_iota(jnp.int32, (K_TILE, K_TILE), 0)
    ci = lax.broadcasted_iota(jnp.int32, (K_TILE, K_TILE), 1)
    neg_upper = jnp.where(ri > ci, -1.0, 0.0).astype(bf16)
    o = _sb_attention(q.reshape(b, s, d), k.reshape(b, s, d),
                      v.reshape(b, s, d), neg_upper)
    h = _out_proj(o.reshape(n, d), h, sb_w_o[0].astype(bf16), row(sb_post_g[0]))
    h = ffn(h, 0)

    w_f = jnp.zeros((d, LANES), bf16).at[:, :N_HEADS].set(
        w_kvf[:, 2 * d:].astype(bf16))
    b_fp = jnp.zeros((1, LANES), f32).at[0, :N_HEADS].set(b_f.astype(f32))
    k, v, q, fcum = _kvq_proj(h, row(kv_norm_g), row(fox_pre_g[0]),
                              w_kvf[:, :2 * d].astype(bf16), w_f, b_fp,
                              fox_w_q[0].astype(bf16), seq=s)
    fcum_c = fcum.reshape(b, s, LANES)
    fcum_r = jnp.transpose(fcum_c[:, :, :N_HEADS], (0, 2, 1)).reshape(
        b, N_HEADS, s // K_TILE, K_TILE)
    o = _fox_attention(q.reshape(b, s, d), k.reshape(b, s, d),
                       v.reshape(b, s, d), fcum_c, fcum_r)
    h = _out_proj(o.reshape(n, d), h, fox_w_o[0].astype(bf16),
                  row(fox_post_g[0]))
    h = ffn(h, 1)
    return h.reshape(b, s, d)
```

```python
import functools
import math

import jax
import jax.numpy as jnp
from jax import lax
from jax.experimental import pallas as pl
from jax.experimental.pallas import tpu as pltpu

D_MODEL = 1024
N_HEADS = 16
HEAD_DIM = D_MODEL // N_HEADS
D_FF = 2816
RMS_EPS = 1e-6
LOG2E = math.log2(math.e)
Q_SCALE = HEAD_DIM ** -0.5 * LOG2E

LANES = 128
SUBLANES = 8
ROW_TILE = 512
Q_SUPER = 512
K_TILE = 256
FF_CHUNK = 256
NEG_BIG = -1e30
VMEM_LIMIT = 56 * 1024 * 1024

f32 = jnp.float32
bf16 = jnp.bfloat16


def _const_spec(shape):
    return pl.BlockSpec(shape, lambda *_: (0,) * len(shape),
                        pipeline_mode=pl.Buffered(1))


def _params(n_axes):
    return pltpu.CompilerParams(
        dimension_semantics=("arbitrary",) * n_axes,
        vmem_limit_bytes=VMEM_LIMIT)


def _inv_rms(x):
    return lax.rsqrt(jnp.mean(x * x, axis=-1, keepdims=True) + RMS_EPS)


def _dot(a, b):
    return jnp.dot(a, b, preferred_element_type=f32)


def _dot_nt(a, b):
    return lax.dot_general(a, b, (((1,), (1,)), ((), ())),
                           preferred_element_type=f32)


def _qkv_kernel(x_ref, g_ref, w_ref, q_ref, k_ref, v_ref):
    x = x_ref[...]
    a = (x * _inv_rms(x) * g_ref[...]).astype(bf16)
    d = D_MODEL
    q_ref[...] = (_dot(a, w_ref[:, 0:d]) * Q_SCALE).astype(bf16)
    k_ref[...] = _dot(a, w_ref[:, d:2 * d]).astype(bf16)
    v_ref[...] = _dot(a, w_ref[:, 2 * d:3 * d]).astype(bf16)


def _qkv_proj(h, g, w):
    n, d = h.shape
    row = pl.BlockSpec((ROW_TILE, d), lambda i: (i, 0))
    out = jax.ShapeDtypeStruct((n, d), bf16)
    return pl.pallas_call(
        _qkv_kernel,
        grid=(n // ROW_TILE,),
        in_specs=[row, _const_spec((1, d)), _const_spec((d, 3 * d))],
        out_specs=[row, row, row],
        out_shape=[out, out, out],
        compiler_params=_params(1),
        name="qkv_proj",
    )(h, g, w)


def _kvq_kernel(x_ref, gkv_ref, gq_ref, wkv_ref, wf_ref, bf_ref, wq_ref,
                k_ref, v_ref, q_ref, fc_ref, carry_ref, *, tiles_per_seq):
    d = D_MODEL
    x = x_ref[...]
    xn = x * _inv_rms(x)
    s_in = (xn * gkv_ref[...]).astype(bf16)
    a = (xn * gq_ref[...]).astype(bf16)
    k_ref[...] = _dot(s_in, wkv_ref[:, 0:d]).astype(bf16)
    v_ref[...] = _dot(s_in, wkv_ref[:, d:2 * d]).astype(bf16)
    q_ref[...] = (_dot(a, wq_ref[...]) * Q_SCALE).astype(bf16)

    fl = _dot(s_in, wf_ref[...]) + bf_ref[...]
    log_f = (jnp.minimum(fl, 0.0)
             - jnp.log(1.0 + jnp.exp(-jnp.abs(fl)))) * LOG2E
    hi = log_f.astype(bf16)
    lo = (log_f - hi.astype(f32)).astype(bf16)
    r = lax.broadcasted_iota(jnp.int32, (ROW_TILE, ROW_TILE), 0)
    c = lax.broadcasted_iota(jnp.int32, (ROW_TILE, ROW_TILE), 1)
    tri = (r >= c).astype(bf16)
    csum = _dot(tri, hi) + _dot(tri, lo)

    @pl.when(pl.program_id(0) % tiles_per_seq == 0)
    def _():
        carry_ref[...] = jnp.zeros_like(carry_ref)

    fcum = csum + carry_ref[0:1, :]
    fc_ref[...] = fcum
    carry_ref[...] = jnp.broadcast_to(fcum[ROW_TILE - 1:ROW_TILE, :],
                                      carry_ref.shape)


def _kvq_proj(h, gkv, gq, wkv, wf, bfp, wq, *, seq):
    n, d = h.shape
    row = pl.BlockSpec((ROW_TILE, d), lambda i: (i, 0))
    out = jax.ShapeDtypeStruct((n, d), bf16)
    return pl.pallas_call(
        functools.partial(_kvq_kernel, tiles_per_seq=seq // ROW_TILE),
        grid=(n // ROW_TILE,),
        in_specs=[row, _const_spec((1, d)), _const_spec((1, d)),
                  _const_spec((d, 2 * d)), _const_spec((d, LANES)),
                  _const_spec((1, LANES)), _const_spec((d, d))],
        out_specs=[row, row, row,
                   pl.BlockSpec((ROW_TILE, LANES), lambda i: (i, 0))],
        out_shape=[out, out, out, jax.ShapeDtypeStruct((n, LANES), f32)],
        scratch_shapes=[pltpu.VMEM((SUBLANES, LANES), f32)],
        compiler_params=_params(1),
        name="kvq_proj",
    )(h, gkv, gq, wkv, wf, bfp, wq)


def _split_q(q_ref, qs_ref):
    q2 = q_ref[0].astype(f32)
    lane = lax.broadcasted_iota(jnp.int32, q2.shape, 1)
    qs_ref[0] = jnp.where(lane < HEAD_DIM, q2, 0.0).astype(bf16)
    qs_ref[1] = jnp.where(lane < HEAD_DIM, 0.0, q2).astype(bf16)


def _neg_abs(x):
    bits = lax.bitcast_convert_type(x, jnp.uint32) | jnp.uint32(0x80000000)
    return lax.bitcast_convert_type(bits, f32)


def _block_mask(rows, cols, offset, strict):
    r = lax.broadcasted_iota(jnp.int32, (rows, cols), 0) + offset
    c = lax.broadcasted_iota(jnp.int32, (rows, cols), 1)
    return c < r if strict else c <= r


def _tile_row(i):
    return pl.multiple_of(i * K_TILE, K_TILE)


def _next_tile(last, qs, kj_b, n_super):
    tiles = Q_SUPER // K_TILE
    nqs = jnp.where(last, jnp.minimum(qs + 1, n_super - 1), qs)
    nkj = jnp.where(last, nqs * tiles + tiles - 1, kj_b - 1)
    return pl.multiple_of(nqs * Q_SUPER, Q_SUPER), nkj


def _lo_half(rows):
    return lax.broadcasted_iota(jnp.int32, (rows, LANES), 1) < HEAD_DIM


def _lower_rows(x, fill):
    r = lax.broadcasted_iota(jnp.int32, (Q_SUPER, x.shape[1]), 0)
    return jnp.where(r >= K_TILE, jnp.concatenate([x, x], axis=0), fill)


def _put_pair(ref, vals):
    ref[0] = vals[0]
    ref[1] = vals[1]


def _sb_attn_kernel(q_ref, k_ref, v_ref, nu_ref, o_ref,
                    qs_ref, z_ref, w_ref, acc_ref, *, seq):
    n_super = seq // Q_SUPER
    tiles = Q_SUPER // K_TILE
    lane = lax.broadcasted_iota(jnp.int32, (Q_SUPER, LANES), 1)
    lo_half = lane < HEAD_DIM

    def qk(qrow, kj):
        kt = k_ref[0, pl.ds(_tile_row(kj), K_TILE), :]
        return tuple(_dot_nt(qs_ref[h, pl.ds(qrow, Q_SUPER), :], kt)
                     for h in range(2))

    def weights(z, survs, mask):
        ws, new_survs = [], []
        for h in range(2):
            sp = (jnp.maximum(z[h], 0.0)
                  + jnp.log(1.0 + jnp.exp2(_neg_abs(z[h]))) * LOG2E)
            if mask is not None:
                sp = jnp.where(mask, sp, 0.0)
            later = _dot(sp.astype(bf16), nu_ref[...])
            logw = (z[h] - sp) + later + survs[h]
            if mask is not None:
                logw = jnp.where(mask, logw, NEG_BIG)
            ws.append(jnp.exp2(logw).astype(bf16))
            new_survs.append(survs[h] - jnp.sum(sp, axis=-1, keepdims=True))
        return tuple(ws), tuple(new_survs)

    def pv(ws, kj):
        vt = v_ref[0, pl.ds(_tile_row(kj), K_TILE), :]
        return jnp.where(_lo_half(ws[0].shape[0]),
                         _dot(ws[0], vt), _dot(ws[1], vt))

    _split_q(q_ref, qs_ref)
    acc_ref[...] = jnp.zeros_like(acc_ref)
    w_ref[...] = jnp.zeros_like(w_ref)
    _put_pair(z_ref, qk(0, tiles - 1))

    def super_tile(qs, carry):
        q0 = pl.multiple_of(qs * Q_SUPER, Q_SUPER)
        kj_a = qs * tiles + 1
        kj_b = qs * tiles
        zb = qk(q0, kj_b)
        prev = pl.multiple_of(jnp.maximum(qs - 1, 0) * Q_SUPER, Q_SUPER)
        o_ref[0, pl.ds(prev, Q_SUPER), :] = (
            acc_ref[...] + pv((w_ref[0], w_ref[1]), 0)).astype(o_ref.dtype)
        zero = jnp.zeros((K_TILE, 1), f32)
        w_a, survs = weights(
            (z_ref[0, K_TILE:, :], z_ref[1, K_TILE:, :]), (zero, zero),
            _block_mask(K_TILE, K_TILE, 0, strict=True))
        survs = tuple(_lower_rows(s, 0.0) for s in survs)
        _put_pair(z_ref, qk(*_next_tile(qs == 0, qs, kj_b, n_super)))
        w_b, survs = weights(zb, survs,
                             _block_mask(Q_SUPER, K_TILE, 0, strict=True))
        acc_ref[:K_TILE, :] = jnp.zeros((K_TILE, LANES), f32)
        acc_ref[K_TILE:, :] = pv(w_a, kj_a)
        _put_pair(w_ref, w_b)

        def pair(j, survs):
            kj_a = qs * tiles - 1 - 2 * j
            kj_b = kj_a - 1
            zb = qk(q0, kj_b)
            w_a, survs = weights((z_ref[0], z_ref[1]), survs, None)
            acc_ref[...] += pv((w_ref[0], w_ref[1]), kj_a + 1)
            _put_pair(z_ref, qk(*_next_tile(j == qs - 1, qs, kj_b, n_super)))
            w_b, survs = weights(zb, survs, None)
            acc_ref[...] += pv(w_a, kj_a)
            _put_pair(w_ref, w_b)
            return survs

        lax.fori_loop(0, qs, pair, survs)
        return carry

    lax.fori_loop(0, n_super, super_tile, 0)
    o_ref[0, pl.ds(seq - Q_SUPER, Q_SUPER), :] = (
        acc_ref[...] + pv((w_ref[0], w_ref[1]), 0)).astype(o_ref.dtype)


def _sb_attention(q, k, v, neg_upper):
    b, s, d = q.shape
    blk = pl.BlockSpec((1, s, LANES), lambda bi, hp: (bi, 0, hp))
    return pl.pallas_call(
        functools.partial(_sb_attn_kernel, seq=s),
        grid=(b, d // LANES),
        in_specs=[blk, blk, blk, _const_spec((K_TILE, K_TILE))],
        out_specs=blk,
        out_shape=jax.ShapeDtypeStruct((b, s, d), bf16),
        scratch_shapes=[pltpu.VMEM((2, s, LANES), bf16),
                        pltpu.VMEM((2, Q_SUPER, K_TILE), f32),
                        pltpu.VMEM((2, Q_SUPER, K_TILE), bf16),
                        pltpu.VMEM((Q_SUPER, LANES), f32)],
        compiler_params=_params(2),
        name="sb_attention",
    )(q, k, v, neg_upper)


def _fox_attn_kernel(q_ref, k_ref, v_ref, fc_ref, fr_ref, o_ref,
                     qs_ref, z_ref, p_ref, num_ref, den_ref, *, seq):
    hp = pl.program_id(1)
    n_super = seq // Q_SUPER
    tiles = Q_SUPER // K_TILE
    lane = lax.broadcasted_iota(jnp.int32, (Q_SUPER, LANES), 1)
    lo_half = lane < HEAD_DIM
    klane = lax.broadcasted_iota(jnp.int32, (K_TILE, LANES), 1)
    own_half = (klane < HEAD_DIM, klane >= HEAD_DIM)

    def qk(qrow, kj):
        kt = k_ref[0, pl.ds(_tile_row(kj), K_TILE), :]
        return tuple(_dot_nt(qs_ref[h, pl.ds(qrow, Q_SUPER), :], kt)
                     for h in range(2))

    def probs(z, fq, kj, ms, masked):
        rows = z[0].shape[0]
        ps, alphas, new_ms = [], [], []
        for h in range(2):
            fk = fr_ref[0, 2 * hp + h, pl.ds(kj, 1), :]
            parts = []
            for c in range(0, K_TILE, LANES):
                s = z[h][:, c:c + LANES] + fq[h] - fk[:, c:c + LANES]
                if masked:
                    s = jnp.where(_block_mask(rows, LANES, -c, strict=False),
                                  s, NEG_BIG)
                parts.append(s)
            row_max = jnp.max(functools.reduce(jnp.maximum, parts),
                              axis=-1, keepdims=True)
            m_new = jnp.maximum(ms[h], row_max)
            ps.append(jnp.concatenate(
                [jnp.exp2(s - m_new) for s in parts], axis=1).astype(bf16))
            alphas.append(jnp.exp2(ms[h] - m_new))
            new_ms.append(m_new)
        return tuple(ps), tuple(alphas), tuple(new_ms)

    def pv(ps, kj):
        vt = v_ref[0, pl.ds(_tile_row(kj), K_TILE), :].astype(f32)
        pvs = [_dot(ps[h], jnp.where(own_half[h], vt, 1.0).astype(bf16))
               for h in range(2)]
        lo = _lo_half(ps[0].shape[0])
        return jnp.where(lo, pvs[0], pvs[1]), jnp.where(lo, pvs[1], pvs[0])

    def rescaled(alphas, pvn, pvd):
        return (jnp.where(lo_half, alphas[0], alphas[1]) * num_ref[...] + pvn,
                jnp.where(lo_half, alphas[1], alphas[0]) * den_ref[...] + pvd)

    def accumulate(alphas, pvn, pvd):
        num, den = rescaled(alphas, pvn, pvd)
        num_ref[...] = num
        den_ref[...] = den

    def finish(alphas, rows):
        num, den = rescaled(alphas, *pv((p_ref[0], p_ref[1]), 0))
        o_ref[0, pl.ds(rows, Q_SUPER), :] = (
            num / pltpu.roll(den, HEAD_DIM, 1)).astype(o_ref.dtype)

    _split_q(q_ref, qs_ref)
    num_ref[...] = jnp.zeros_like(num_ref)
    den_ref[...] = jnp.ones_like(den_ref)
    p_ref[...] = jnp.zeros_like(p_ref)
    _put_pair(z_ref, qk(0, tiles - 1))
    one = jnp.ones((Q_SUPER, LANES), f32)

    def super_tile(qs, last_alphas):
        q0 = pl.multiple_of(qs * Q_SUPER, Q_SUPER)
        kj_a = qs * tiles + 1
        kj_b = qs * tiles
        zb = qk(q0, kj_b)
        finish(last_alphas,
               pl.multiple_of(jnp.maximum(qs - 1, 0) * Q_SUPER, Q_SUPER))
        fc = fc_ref[0, pl.ds(q0, Q_SUPER), :]
        fq = tuple(
            jnp.broadcast_to(
                jnp.sum(jnp.where(lane == 2 * hp + h, fc, 0.0), axis=-1,
                        keepdims=True), (Q_SUPER, LANES)) for h in range(2))
        m0 = jnp.full((K_TILE, LANES), NEG_BIG, f32)
        p_a, _, ms = probs((z_ref[0, K_TILE:, :], z_ref[1, K_TILE:, :]),
                           tuple(f[K_TILE:, :] for f in fq), kj_a, (m0, m0),
                           True)
        ms = tuple(_lower_rows(m, NEG_BIG) for m in ms)
        _put_pair(z_ref, qk(*_next_tile(qs == 0, qs, kj_b, n_super)))
        p_b, al_b, ms = probs(zb, fq, kj_b, ms, True)
        pvn, pvd = pv(p_a, kj_a)
        num_ref[:K_TILE, :] = jnp.zeros((K_TILE, LANES), f32)
        num_ref[K_TILE:, :] = pvn
        den_ref[:K_TILE, :] = jnp.zeros((K_TILE, LANES), f32)
        den_ref[K_TILE:, :] = pvd
        _put_pair(p_ref, p_b)

        def pair(j, carry):
            ms, al_prev = carry
            kj_a = qs * tiles - 1 - 2 * j
            kj_b = kj_a - 1
            zb = qk(q0, kj_b)
            p_a, al_a, ms = probs((z_ref[0], z_ref[1]), fq, kj_a, ms, False)
            accumulate(al_prev, *pv((p_ref[0], p_ref[1]), kj_a + 1))
            _put_pair(z_ref, qk(*_next_tile(j == qs - 1, qs, kj_b, n_super)))
            p_b, al_b, ms = probs(zb, fq, kj_b, ms, False)
            accumulate(al_a, *pv(p_a, kj_a))
            _put_pair(p_ref, p_b)
            return ms, al_b

        _, last_alphas = lax.fori_loop(0, qs, pair, (ms, al_b))
        return last_alphas

    last_alphas = lax.fori_loop(0, n_super, super_tile, (one, one))
    finish(last_alphas, seq - Q_SUPER)


def _fox_attention(q, k, v, fcum_c, fcum_r):
    b, s, d = q.shape
    blk = pl.BlockSpec((1, s, LANES), lambda bi, hp: (bi, 0, hp))
    nk = s // K_TILE
    return pl.pallas_call(
        functools.partial(_fox_attn_kernel, seq=s),
        grid=(b, d // LANES),
        in_specs=[blk, blk, blk,
                  pl.BlockSpec((1, s, LANES), lambda bi, hp: (bi, 0, 0)),
                  pl.BlockSpec((1, N_HEADS, nk, K_TILE),
                               lambda bi, hp: (bi, 0, 0, 0))],
        out_specs=blk,
        out_shape=jax.ShapeDtypeStruct((b, s, d), bf16),
        scratch_shapes=[pltpu.VMEM((2, s, LANES), bf16),
                        pltpu.VMEM((2, Q_SUPER, K_TILE), f32),
                        pltpu.VMEM((2, Q_SUPER, K_TILE), bf16),
                        pltpu.VMEM((Q_SUPER, LANES), f32),
                        pltpu.VMEM((Q_SUPER, LANES), f32)],
        compiler_params=_params(2),
        name="fox_attention",
    )(q, k, v, fcum_c, fcum_r)


def _gelu_tanh(x):
    c = 0.7978845608028654
    return 0.5 * x * (1.0 + jnp.tanh(c * (x + 0.044715 * (x * x * x))))


def _ffn_kernel(o_ref, h_ref, wo_ref, go_ref, g1_ref, wup_ref, cw_ref, cb_ref,
                wdn_ref, g2_ref, out_ref, act_ref, halo_ref, *, tiles_per_seq):
    y = _dot(o_ref[...], wo_ref[...])
    x = h_ref[...] + y * _inv_rms(y) * go_ref[...]
    a = (x * _inv_rms(x) * g1_ref[...]).astype(bf16)

    @pl.when(pl.program_id(0) % tiles_per_seq == 0)
    def _():
        halo_ref[...] = jnp.zeros_like(halo_ref)

    def conv(cols):
        hdn = _dot(a, wup_ref[:, cols])
        xx = jnp.concatenate([halo_ref[:, cols], hdn], axis=0)
        halo_ref[:, cols] = hdn[ROW_TILE - SUBLANES:, :]
        prev1 = pltpu.roll(xx, 1, 0)[SUBLANES:, :]
        prev2 = pltpu.roll(xx, 2, 0)[SUBLANES:, :]
        return (cw_ref[2:3, cols] * hdn + cw_ref[1:2, cols] * prev1
                + cw_ref[0:1, cols] * prev2 + cb_ref[:, cols])

    for c in range(D_FF // FF_CHUNK):
        gate = conv(slice(c * FF_CHUNK, (c + 1) * FF_CHUNK))
        up = conv(slice(D_FF + c * FF_CHUNK, D_FF + (c + 1) * FF_CHUNK))
        act_ref[:, c * FF_CHUNK:(c + 1) * FF_CHUNK] = (
            _gelu_tanh(gate) * up).astype(bf16)

    f = _dot(act_ref[...], wdn_ref[...])
    out_ref[...] = x + f * _inv_rms(f) * g2_ref[...]


def _attn_out_ffn(o, h, wo, go, g1, wup, cw, cb, wdn, g2, *, seq):
    n, d = h.shape
    row = pl.BlockSpec((ROW_TILE, d), lambda i: (i, 0))
    return pl.pallas_call(
        functools.partial(_ffn_kernel, tiles_per_seq=seq // ROW_TILE),
        grid=(n // ROW_TILE,),
        in_specs=[row, row, _const_spec((d, d)), _const_spec((1, d)),
                  _const_spec((1, d)), _const_spec((d, 2 * D_FF)),
                  _const_spec((3, 2 * D_FF)), _const_spec((1, 2 * D_FF)),
                  _const_spec((D_FF, d)), _const_spec((1, d))],
        out_specs=row,
        out_shape=jax.ShapeDtypeStruct((n, d), f32),
        scratch_shapes=[pltpu.VMEM((ROW_TILE, D_FF), bf16),
                        pltpu.VMEM((SUBLANES, 2 * D_FF), f32)],
        compiler_params=_params(1),
        name="attn_out_ffn",
    )(o, h, wo, go, g1, wup, cw, cb, wdn, g2)


def kernel(x, sb_pre_g, sb_w_qkv, sb_w_o, sb_post_g, kv_norm_g, w_kvf, b_f,
           fox_pre_g, fox_w_q, fox_w_o, fox_post_g,
           ffn_pre_g, w_up, conv_w, conv_b, w_down, ffn_post_g):
    b, s, d = x.shape
    n = b * s
    assert d == D_MODEL and s % ROW_TILE == 0 and s % Q_SUPER == 0
    assert sb_pre_g.shape[0] == 1 and fox_pre_g.shape[0] == 1

    def row(g):
        return g.reshape(1, -1).astype(f32)

    def out_ffn(o, h, wo, go, layer):
        return _attn_out_ffn(
            o.reshape(n, d), h, wo.astype(bf16), row(go),
            row(ffn_pre_g[layer]), w_up[layer].astype(bf16),
            conv_w[layer].astype(f32), row(conv_b[layer]),
            w_down[layer].astype(bf16), row(ffn_post_g[layer]), seq=s)

    h = x.reshape(n, d)

    q, k, v = _qkv_proj(h, row(sb_pre_g[0]), sb_w_qkv[0].astype(bf16))
    ri = lax.broadcasted_iota(jnp.int32, (K_TILE, K_TILE), 0)
    ci = lax.broadcasted_iota(jnp.int32, (K_TILE, K_TILE), 1)
    neg_upper = jnp.where(ri > ci, -1.0, 0.0).astype(bf16)
    o = _sb_attention(q.reshape(b, s, d), k.reshape(b, s, d),
                      v.reshape(b, s, d), neg_upper)
    h = out_ffn(o, h, sb_w_o[0], sb_post_g[0], 0)

    w_f = jnp.zeros((d, LANES), bf16).at[:, :N_HEADS].set(
        w_kvf[:, 2 * d:].astype(bf16))
    b_fp = jnp.zeros((1, LANES), f32).at[0, :N_HEADS].set(b_f.astype(f32))
    k, v, q, fcum = _kvq_proj(h, row(kv_norm_g), row(fox_pre_g[0]),
                              w_kvf[:, :2 * d].astype(bf16), w_f, b_fp,
                              fox_w_q[0].astype(bf16), seq=s)
    fcum_c = fcum.reshape(b, s, LANES)
    fcum_r = jnp.transpose(fcum_c[:, :, :N_HEADS], (0, 2, 1)).reshape(
        b, N_HEADS, s // K_TILE, K_TILE)
    o = _fox_attention(q.reshape(b, s, d), k.reshape(b, s, d),
                       v.reshape(b, s, d), fcum_c, fcum_r)
    h = out_ffn(o, h, fox_w_o[0], fox_post_g[0], 1)
    return h.reshape(b, s, d)
```

```python
import functools
import math

import jax
import jax.numpy as jnp
from jax import lax
from jax.experimental import pallas as pl
from jax.experimental.pallas import tpu as pltpu

D_MODEL = 1024
N_HEADS = 16
HEAD_DIM = D_MODEL // N_HEADS
D_FF = 2816
RMS_EPS = 1e-6
LOG2E = math.log2(math.e)
Q_SCALE = HEAD_DIM ** -0.5 * LOG2E

LANES = 128
SUBLANES = 8
ROW_TILE = 512
Q_SUPER = 512
K_TILE = 256
FF_CHUNK = 256
NEG_BIG = -1e30
VMEM_LIMIT = 56 * 1024 * 1024

f32 = jnp.float32
bf16 = jnp.bfloat16


def _const_spec(shape):
    return pl.BlockSpec(shape, lambda *_: (0,) * len(shape),
                        pipeline_mode=pl.Buffered(1))


def _params(n_axes):
    return pltpu.CompilerParams(
        dimension_semantics=("arbitrary",) * n_axes,
        vmem_limit_bytes=VMEM_LIMIT)


def _inv_rms(x):
    return lax.rsqrt(jnp.mean(x * x, axis=-1, keepdims=True) + RMS_EPS)


def _dot(a, b):
    return jnp.dot(a, b, preferred_element_type=f32)


def _dot_nt(a, b):
    return lax.dot_general(a, b, (((1,), (1,)), ((), ())),
                           preferred_element_type=f32)


def _qkv_kernel(x_ref, g_ref, w_ref, q_ref, k_ref, v_ref):
    x = x_ref[...]
    a = (x * _inv_rms(x) * g_ref[...]).astype(bf16)
    d = D_MODEL
    q_ref[...] = (_dot(a, w_ref[:, 0:d]) * Q_SCALE).astype(bf16)
    k_ref[...] = _dot(a, w_ref[:, d:2 * d]).astype(bf16)
    v_ref[...] = _dot(a, w_ref[:, 2 * d:3 * d]).astype(bf16)


def _qkv_proj(h, g, w):
    n, d = h.shape
    row = pl.BlockSpec((ROW_TILE, d), lambda i: (i, 0))
    out = jax.ShapeDtypeStruct((n, d), bf16)
    return pl.pallas_call(
        _qkv_kernel,
        grid=(n // ROW_TILE,),
        in_specs=[row, _const_spec((1, d)), _const_spec((d, 3 * d))],
        out_specs=[row, row, row],
        out_shape=[out, out, out],
        compiler_params=_params(1),
        name="qkv_proj",
    )(h, g, w)


def _kvq_kernel(x_ref, gkv_ref, gq_ref, wkv_ref, wf_ref, bf_ref, wq_ref,
                k_ref, v_ref, q_ref, fc_ref, carry_ref, *, tiles_per_seq):
    d = D_MODEL
    x = x_ref[...]
    xn = x * _inv_rms(x)
    s_in = (xn * gkv_ref[...]).astype(bf16)
    a = (xn * gq_ref[...]).astype(bf16)
    k_ref[...] = _dot(s_in, wkv_ref[:, 0:d]).astype(bf16)
    v_ref[...] = _dot(s_in, wkv_ref[:, d:2 * d]).astype(bf16)
    q_ref[...] = (_dot(a, wq_ref[...]) * Q_SCALE).astype(bf16)

    fl = _dot(s_in, wf_ref[...]) + bf_ref[...]
    log_f = (jnp.minimum(fl, 0.0)
             - jnp.log(1.0 + jnp.exp(-jnp.abs(fl)))) * LOG2E
    hi = log_f.astype(bf16)
    lo = (log_f - hi.astype(f32)).astype(bf16)
    r = lax.broadcasted_iota(jnp.int32, (ROW_TILE, ROW_TILE), 0)
    c = lax.broadcasted_iota(jnp.int32, (ROW_TILE, ROW_TILE), 1)
    tri = (r >= c).astype(bf16)
    csum = _dot(tri, hi) + _dot(tri, lo)

    @pl.when(pl.program_id(0) % tiles_per_seq == 0)
    def _():
        carry_ref[...] = jnp.zeros_like(carry_ref)

    fcum = csum + carry_ref[0:1, :]
    fc_ref[...] = fcum
    carry_ref[...] = jnp.broadcast_to(fcum[ROW_TILE - 1:ROW_TILE, :],
                                      carry_ref.shape)


def _kvq_proj(h, gkv, gq, wkv, wf, bfp, wq, *, seq):
    n, d = h.shape
    row = pl.BlockSpec((ROW_TILE, d), lambda i: (i, 0))
    out = jax.ShapeDtypeStruct((n, d), bf16)
    return pl.pallas_call(
        functools.partial(_kvq_kernel, tiles_per_seq=seq // ROW_TILE),
        grid=(n // ROW_TILE,),
        in_specs=[row, _const_spec((1, d)), _const_spec((1, d)),
                  _const_spec((d, 2 * d)), _const_spec((d, LANES)),
                  _const_spec((1, LANES)), _const_spec((d, d))],
        out_specs=[row, row, row,
                   pl.BlockSpec((ROW_TILE, LANES), lambda i: (i, 0))],
        out_shape=[out, out, out, jax.ShapeDtypeStruct((n, LANES), f32)],
        scratch_shapes=[pltpu.VMEM((SUBLANES, LANES), f32)],
        compiler_params=_params(1),
        name="kvq_proj",
    )(h, gkv, gq, wkv, wf, bfp, wq)


def _split_q(q_ref, qs_ref):
    q2 = q_ref[0].astype(f32)
    lane = lax.broadcasted_iota(jnp.int32, q2.shape, 1)
    qs_ref[0] = jnp.where(lane < HEAD_DIM, q2, 0.0).astype(bf16)
    qs_ref[1] = jnp.where(lane < HEAD_DIM, 0.0, q2).astype(bf16)


def _neg_abs(x):
    bits = lax.bitcast_convert_type(x, jnp.uint32) | jnp.uint32(0x80000000)
    return lax.bitcast_convert_type(bits, f32)


def _block_mask(rows, cols, offset, strict):
    r = lax.broadcasted_iota(jnp.int32, (rows, cols), 0) + offset
    c = lax.broadcasted_iota(jnp.int32, (rows, cols), 1)
    return c < r if strict else c <= r


def _tile_row(i):
    return pl.multiple_of(i * K_TILE, K_TILE)


def _next_tile(last, qs, kj_b, n_super):
    tiles = Q_SUPER // K_TILE
    nqs = jnp.where(last, jnp.minimum(qs + 1, n_super - 1), qs)
    nkj = jnp.where(last, nqs * tiles + tiles - 1, kj_b - 1)
    return pl.multiple_of(nqs * Q_SUPER, Q_SUPER), nkj


def _lo_half(rows):
    return lax.broadcasted_iota(jnp.int32, (rows, LANES), 1) < HEAD_DIM


def _lower_rows(x, fill):
    r = lax.broadcasted_iota(jnp.int32, (Q_SUPER, x.shape[1]), 0)
    return jnp.where(r >= K_TILE, jnp.concatenate([x, x], axis=0), fill)


def _put_pair(ref, vals):
    ref[0] = vals[0]
    ref[1] = vals[1]


def _sb_attn_kernel(q_ref, k_ref, v_ref, nu_ref, o_ref,
                    qs_ref, z_ref, w_ref, acc_ref, *, seq):
    n_super = seq // Q_SUPER
    tiles = Q_SUPER // K_TILE
    lane = lax.broadcasted_iota(jnp.int32, (Q_SUPER, LANES), 1)
    lo_half = lane < HEAD_DIM

    def qk(qrow, kj):
        kt = k_ref[0, pl.ds(_tile_row(kj), K_TILE), :]
        return tuple(_dot_nt(qs_ref[h, pl.ds(qrow, Q_SUPER), :], kt)
                     for h in range(2))

    def weights(z, survs, mask):
        ws, new_survs = [], []
        for h in range(2):
            sp = (jnp.maximum(z[h], 0.0)
                  + jnp.log(1.0 + jnp.exp2(_neg_abs(z[h]))) * LOG2E)
            if mask is not None:
                sp = jnp.where(mask, sp, 0.0)
            later = _dot(sp.astype(bf16), nu_ref[...])
            logw = (z[h] - sp) + later + survs[h]
            if mask is not None:
                logw = jnp.where(mask, logw, NEG_BIG)
            ws.append(jnp.exp2(logw).astype(bf16))
            new_survs.append(survs[h] - jnp.sum(sp, axis=-1, keepdims=True))
        return tuple(ws), tuple(new_survs)

    def pv(ws, kj):
        vt = v_ref[0, pl.ds(_tile_row(kj), K_TILE), :]
        return jnp.where(_lo_half(ws[0].shape[0]),
                         _dot(ws[0], vt), _dot(ws[1], vt))

    _split_q(q_ref, qs_ref)
    acc_ref[...] = jnp.zeros_like(acc_ref)
    w_ref[...] = jnp.zeros_like(w_ref)
    _put_pair(z_ref, qk(0, tiles - 1))

    def super_tile(qs, carry):
        q0 = pl.multiple_of(qs * Q_SUPER, Q_SUPER)
        kj_a = qs * tiles + 1
        kj_b = qs * tiles
        zb = qk(q0, kj_b)
        prev = pl.multiple_of(jnp.maximum(qs - 1, 0) * Q_SUPER, Q_SUPER)
        o_ref[0, pl.ds(prev, Q_SUPER), :] = (
            acc_ref[...] + pv((w_ref[0], w_ref[1]), 0)).astype(o_ref.dtype)
        zero = jnp.zeros((K_TILE, 1), f32)
        w_a, survs = weights(
            (z_ref[0, K_TILE:, :], z_ref[1, K_TILE:, :]), (zero, zero),
            _block_mask(K_TILE, K_TILE, 0, strict=True))
        survs = tuple(_lower_rows(s, 0.0) for s in survs)
        _put_pair(z_ref, qk(*_next_tile(qs == 0, qs, kj_b, n_super)))
        w_b, survs = weights(zb, survs,
                             _block_mask(Q_SUPER, K_TILE, 0, strict=True))
        acc_ref[:K_TILE, :] = jnp.zeros((K_TILE, LANES), f32)
        acc_ref[K_TILE:, :] = pv(w_a, kj_a)
        _put_pair(w_ref, w_b)

        def pair(j, survs):
            kj_a = qs * tiles - 1 - 2 * j
            kj_b = kj_a - 1
            zb = qk(q0, kj_b)
            w_a, survs = weights((z_ref[0], z_ref[1]), survs, None)
            acc_ref[...] += pv((w_ref[0], w_ref[1]), kj_a + 1)
            _put_pair(z_ref, qk(*_next_tile(j == qs - 1, qs, kj_b, n_super)))
            w_b, survs = weights(zb, survs, None)
            acc_ref[...] += pv(w_a, kj_a)
            _put_pair(w_ref, w_b)
            return survs

        lax.fori_loop(0, qs, pair, survs)
        return carry

    lax.fori_loop(0, n_super, super_tile, 0)
    o_ref[0, pl.ds(seq - Q_SUPER, Q_SUPER), :] = (
        acc_ref[...] + pv((w_ref[0], w_ref[1]), 0)).astype(o_ref.dtype)


def _sb_attention(q, k, v, neg_upper):
    b, s, d = q.shape
    blk = pl.BlockSpec((1, s, LANES), lambda bi, hp: (bi, 0, hp))
    return pl.pallas_call(
        functools.partial(_sb_attn_kernel, seq=s),
        grid=(b, d // LANES),
        in_specs=[blk, blk, blk, _const_spec((K_TILE, K_TILE))],
        out_specs=blk,
        out_shape=jax.ShapeDtypeStruct((b, s, d), bf16),
        scratch_shapes=[pltpu.VMEM((2, s, LANES), bf16),
                        pltpu.VMEM((2, Q_SUPER, K_TILE), f32),
                        pltpu.VMEM((2, Q_SUPER, K_TILE), bf16),
                        pltpu.VMEM((Q_SUPER, LANES), f32)],
        compiler_params=_params(2),
        name="sb_attention",
    )(q, k, v, neg_upper)


def _fox_attn_kernel(q_ref, k_ref, v_ref, fc_ref, fr_ref, swap_ref, o_ref,
                     qs_ref, z_ref, p_ref, num_ref, den_ref, *, seq):
    hp = pl.program_id(1)
    n_super = seq // Q_SUPER
    tiles = Q_SUPER // K_TILE
    lane = lax.broadcasted_iota(jnp.int32, (Q_SUPER, LANES), 1)
    lo_half = lane < HEAD_DIM
    klane = lax.broadcasted_iota(jnp.int32, (K_TILE, LANES), 1)
    own_half = (klane < HEAD_DIM, klane >= HEAD_DIM)

    def qk(qrow, kj):
        kt = k_ref[0, pl.ds(_tile_row(kj), K_TILE), :]
        return tuple(_dot_nt(qs_ref[h, pl.ds(qrow, Q_SUPER), :], kt)
                     for h in range(2))

    def probs(z, fq, kj, ms, masked):
        rows = z[0].shape[0]
        ps, alphas, new_ms = [], [], []
        for h in range(2):
            fk = fr_ref[0, 2 * hp + h, pl.ds(kj, 1), :]
            parts = []
            for c in range(0, K_TILE, LANES):
                s = z[h][:, c:c + LANES] + fq[h] - fk[:, c:c + LANES]
                if masked:
                    s = jnp.where(_block_mask(rows, LANES, -c, strict=False),
                                  s, NEG_BIG)
                parts.append(s)
            row_max = jnp.max(functools.reduce(jnp.maximum, parts),
                              axis=-1, keepdims=True)
            m_new = jnp.maximum(ms[h], row_max)
            ps.append(jnp.concatenate(
                [jnp.exp2(s - m_new) for s in parts], axis=1).astype(bf16))
            alphas.append(jnp.exp2(ms[h] - m_new))
            new_ms.append(m_new)
        return tuple(ps), tuple(alphas), tuple(new_ms)

    def pv(ps, kj):
        vt = v_ref[0, pl.ds(_tile_row(kj), K_TILE), :].astype(f32)
        pvs = [_dot(ps[h], jnp.where(own_half[h], vt, 1.0).astype(bf16))
               for h in range(2)]
        lo = _lo_half(ps[0].shape[0])
        return jnp.where(lo, pvs[0], pvs[1]), jnp.where(lo, pvs[1], pvs[0])

    def rescaled(alphas, pvn, pvd):
        return (jnp.where(lo_half, alphas[0], alphas[1]) * num_ref[...] + pvn,
                jnp.where(lo_half, alphas[1], alphas[0]) * den_ref[...] + pvd)

    def accumulate(alphas, pvn, pvd):
        num, den = rescaled(alphas, pvn, pvd)
        num_ref[...] = num
        den_ref[...] = den

    def finish(alphas, rows):
        num, den = rescaled(alphas, *pv((p_ref[0], p_ref[1]), 0))
        hi = den.astype(bf16)
        rest = den - hi.astype(f32)
        mid = rest.astype(bf16)
        lo = (rest - mid.astype(f32)).astype(bf16)
        swapped = _dot(jnp.concatenate([hi, mid, lo], axis=1), swap_ref[...])
        o_ref[0, pl.ds(rows, Q_SUPER), :] = (num / swapped).astype(o_ref.dtype)

    _split_q(q_ref, qs_ref)
    num_ref[...] = jnp.zeros_like(num_ref)
    den_ref[...] = jnp.ones_like(den_ref)
    p_ref[...] = jnp.zeros_like(p_ref)
    _put_pair(z_ref, qk(0, tiles - 1))
    one = jnp.ones((Q_SUPER, LANES), f32)

    def super_tile(qs, last_alphas):
        q0 = pl.multiple_of(qs * Q_SUPER, Q_SUPER)
        kj_a = qs * tiles + 1
        kj_b = qs * tiles
        zb = qk(q0, kj_b)
        finish(last_alphas,
               pl.multiple_of(jnp.maximum(qs - 1, 0) * Q_SUPER, Q_SUPER))
        fc = fc_ref[0, pl.ds(q0, Q_SUPER), :]
        fq = tuple(
            jnp.broadcast_to(
                jnp.sum(jnp.where(lane == 2 * hp + h, fc, 0.0), axis=-1,
                        keepdims=True), (Q_SUPER, LANES)) for h in range(2))
        m0 = jnp.full((K_TILE, LANES), NEG_BIG, f32)
        p_a, _, ms = probs((z_ref[0, K_TILE:, :], z_ref[1, K_TILE:, :]),
                           tuple(f[K_TILE:, :] for f in fq), kj_a, (m0, m0),
                           True)
        ms = tuple(_lower_rows(m, NEG_BIG) for m in ms)
        _put_pair(z_ref, qk(*_next_tile(qs == 0, qs, kj_b, n_super)))
        p_b, al_b, ms = probs(zb, fq, kj_b, ms, True)
        pvn, pvd = pv(p_a, kj_a)
        num_ref[:K_TILE, :] = jnp.zeros((K_TILE, LANES), f32)
        num_ref[K_TILE:, :] = pvn
        den_ref[:K_TILE, :] = jnp.zeros((K_TILE, LANES), f32)
        den_ref[K_TILE:, :] = pvd
        _put_pair(p_ref, p_b)

        def pair(j, carry):
            ms, al_prev = carry
            kj_a = qs * tiles - 1 - 2 * j
            kj_b = kj_a - 1
            zb = qk(q0, kj_b)
            p_a, al_a, ms = probs((z_ref[0], z_ref[1]), fq, kj_a, ms, False)
            accumulate(al_prev, *pv((p_ref[0], p_ref[1]), kj_a + 1))
            _put_pair(z_ref, qk(*_next_tile(j == qs - 1, qs, kj_b, n_super)))
            p_b, al_b, ms = probs(zb, fq, kj_b, ms, False)
            accumulate(al_a, *pv(p_a, kj_a))
            _put_pair(p_ref, p_b)
            return ms, al_b

        _, last_alphas = lax.fori_loop(0, qs, pair, (ms, al_b))
        return last_alphas

    last_alphas = lax.fori_loop(0, n_super, super_tile, (one, one))
    finish(last_alphas, seq - Q_SUPER)


def _fox_attention(q, k, v, fcum_c, fcum_r):
    b, s, d = q.shape
    blk = pl.BlockSpec((1, s, LANES), lambda bi, hp: (bi, 0, hp))
    nk = s // K_TILE
    ri = lax.broadcasted_iota(jnp.int32, (3 * LANES, LANES), 0)
    ci = lax.broadcasted_iota(jnp.int32, (3 * LANES, LANES), 1)
    swap = (ri % LANES == (ci + HEAD_DIM) % LANES).astype(bf16)
    return pl.pallas_call(
        functools.partial(_fox_attn_kernel, seq=s),
        grid=(b, d // LANES),
        in_specs=[blk, blk, blk,
                  pl.BlockSpec((1, s, LANES), lambda bi, hp: (bi, 0, 0)),
                  pl.BlockSpec((1, N_HEADS, nk, K_TILE),
                               lambda bi, hp: (bi, 0, 0, 0)),
                  _const_spec((3 * LANES, LANES))],
        out_specs=blk,
        out_shape=jax.ShapeDtypeStruct((b, s, d), bf16),
        scratch_shapes=[pltpu.VMEM((2, s, LANES), bf16),
                        pltpu.VMEM((2, Q_SUPER, K_TILE), f32),
                        pltpu.VMEM((2, Q_SUPER, K_TILE), bf16),
                        pltpu.VMEM((Q_SUPER, LANES), f32),
                        pltpu.VMEM((Q_SUPER, LANES), f32)],
        compiler_params=_params(2),
        name="fox_attention",
    )(q, k, v, fcum_c, fcum_r, swap)


def _gelu_tanh(x):
    c = 0.7978845608028654
    return 0.5 * x * (1.0 + jnp.tanh(c * (x + 0.044715 * (x * x * x))))


def _ffn_kernel(o_ref, h_ref, wo_ref, go_ref, g1_ref, wup_ref, cw_ref, cb_ref,
                wdn_ref, g2_ref, out_ref, act_ref, halo_ref, *, tiles_per_seq):
    y = _dot(o_ref[...], wo_ref[...])
    x = h_ref[...] + y * _inv_rms(y) * go_ref[...]
    a = (x * _inv_rms(x) * g1_ref[...]).astype(bf16)

    @pl.when(pl.program_id(0) % tiles_per_seq == 0)
    def _():
        halo_ref[...] = jnp.zeros_like(halo_ref)

    def conv(cols):
        hdn = _dot(a, wup_ref[:, cols])
        xx = jnp.concatenate([halo_ref[:, cols], hdn], axis=0)
        halo_ref[:, cols] = hdn[ROW_TILE - SUBLANES:, :]
        prev1 = pltpu.roll(xx, 1, 0)[SUBLANES:, :]
        prev2 = pltpu.roll(xx, 2, 0)[SUBLANES:, :]
        return (cw_ref[2:3, cols] * hdn + cw_ref[1:2, cols] * prev1
                + cw_ref[0:1, cols] * prev2 + cb_ref[:, cols])

    for c in range(D_FF // FF_CHUNK):
        gate = conv(slice(c * FF_CHUNK, (c + 1) * FF_CHUNK))
        up = conv(slice(D_FF + c * FF_CHUNK, D_FF + (c + 1) * FF_CHUNK))
        act_ref[:, c * FF_CHUNK:(c + 1) * FF_CHUNK] = (
            _gelu_tanh(gate) * up).astype(bf16)

    f = _dot(act_ref[...], wdn_ref[...])
    out_ref[...] = x + f * _inv_rms(f) * g2_ref[...]


def _attn_out_ffn(o, h, wo, go, g1, wup, cw, cb, wdn, g2, *, seq):
    n, d = h.shape
    row = pl.BlockSpec((ROW_TILE, d), lambda i: (i, 0))
    return pl.pallas_call(
        functools.partial(_ffn_kernel, tiles_per_seq=seq // ROW_TILE),
        grid=(n // ROW_TILE,),
        in_specs=[row, row, _const_spec((d, d)), _const_spec((1, d)),
                  _const_spec((1, d)), _const_spec((d, 2 * D_FF)),
                  _const_spec((3, 2 * D_FF)), _const_spec((1, 2 * D_FF)),
                  _const_spec((D_FF, d)), _const_spec((1, d))],
        out_specs=row,
        out_shape=jax.ShapeDtypeStruct((n, d), f32),
        scratch_shapes=[pltpu.VMEM((ROW_TILE, D_FF), bf16),
                        pltpu.VMEM((SUBLANES, 2 * D_FF), f32)],
        compiler_params=_params(1),
        name="attn_out_ffn",
    )(o, h, wo, go, g1, wup, cw, cb, wdn, g2)


def kernel(x, sb_pre_g, sb_w_qkv, sb_w_o, sb_post_g, kv_norm_g, w_kvf, b_f,
           fox_pre_g, fox_w_q, fox_w_o, fox_post_g,
           ffn_pre_g, w_up, conv_w, conv_b, w_down, ffn_post_g):
    b, s, d = x.shape
    n = b * s
    assert d == D_MODEL and s % ROW_TILE == 0 and s % Q_SUPER == 0
    assert sb_pre_g.shape[0] == 1 and fox_pre_g.shape[0] == 1

    def row(g):
        return g.reshape(1, -1).astype(f32)

    def out_ffn(o, h, wo, go, layer):
        return _attn_out_ffn(
            o.reshape(n, d), h, wo.astype(bf16), row(go),
            row(ffn_pre_g[layer]), w_up[layer].astype(bf16),
            conv_w[layer].astype(f32), row(conv_b[layer]),
            w_down[layer].astype(bf16), row(ffn_post_g[layer]), seq=s)

    h = x.reshape(n, d)

    q, k, v = _qkv_proj(h, row(sb_pre_g[0]), sb_w_qkv[0].astype(bf16))
    ri = lax.broadcasted_iota(jnp.int32, (K_TILE, K_TILE), 0)
    ci = lax.broadcasted_iota(jnp.int32, (K_TILE, K_TILE), 1)
    neg_upper = jnp.where(ri > ci, -1.0, 0.0).astype(bf16)
    o = _sb_attention(q.reshape(b, s, d), k.reshape(b, s, d),
                      v.reshape(b, s, d), neg_upper)
    h = out_ffn(o, h, sb_w_o[0], sb_post_g[0], 0)

    w_f = jnp.zeros((d, LANES), bf16).at[:, :N_HEADS].set(
        w_kvf[:, 2 * d:].astype(bf16))
    b_fp = jnp.zeros((1, LANES), f32).at[0, :N_HEADS].set(b_f.astype(f32))
    k, v, q, fcum = _kvq_proj(h, row(kv_norm_g), row(fox_pre_g[0]),
                              w_kvf[:, :2 * d].astype(bf16), w_f, b_fp,
                              fox_w_q[0].astype(bf16), seq=s)
    fcum_c = fcum.reshape(b, s, LANES)
    fcum_r = jnp.transpose(fcum_c[:, :, :N_HEADS], (0, 2, 1)).reshape(
        b, N_HEADS, s // K_TILE, K_TILE)
    o = _fox_attention(q.reshape(b, s, d), k.reshape(b, s, d),
                       v.reshape(b, s, d), fcum_c, fcum_r)
    h = out_ffn(o, h, fox_w_o[0], fox_post_g[0], 1)
    return h.reshape(b, s, d)
```

```python
import functools
import math

import jax
import jax.numpy as jnp
from jax import lax
from jax.experimental import pallas as pl
from jax.experimental.pallas import tpu as pltpu

D_MODEL = 1024
N_HEADS = 16
HEAD_DIM = D_MODEL // N_HEADS
D_FF = 2816
RMS_EPS = 1e-6
LOG2E = math.log2(math.e)
Q_SCALE = HEAD_DIM ** -0.5 * LOG2E

LANES = 128
SUBLANES = 8
ROW_TILE = 512
Q_SUPER = 512
K_TILE = 256
FF_CHUNK = 256
ROW_CHUNK = 128
NEG_BIG = -1e30
VMEM_LIMIT = 56 * 1024 * 1024

f32 = jnp.float32
bf16 = jnp.bfloat16


def _const_spec(shape):
    return pl.BlockSpec(shape, lambda *_: (0,) * len(shape),
                        pipeline_mode=pl.Buffered(1))


def _params(n_axes):
    return pltpu.CompilerParams(
        dimension_semantics=("arbitrary",) * n_axes,
        vmem_limit_bytes=VMEM_LIMIT)


def _inv_rms(x):
    return lax.rsqrt(jnp.mean(x * x, axis=-1, keepdims=True) + RMS_EPS)


def _dot(a, b):
    return jnp.dot(a, b, preferred_element_type=f32)


def _dot_nt(a, b):
    return lax.dot_general(a, b, (((1,), (1,)), ((), ())),
                           preferred_element_type=f32)


def _qkv_kernel(x_ref, g_ref, w_ref, q_ref, k_ref, v_ref):
    x = x_ref[...]
    a = (x * _inv_rms(x) * g_ref[...]).astype(bf16)
    d = D_MODEL
    q_ref[...] = (_dot(a, w_ref[:, 0:d]) * Q_SCALE).astype(bf16)
    k_ref[...] = _dot(a, w_ref[:, d:2 * d]).astype(bf16)
    v_ref[...] = _dot(a, w_ref[:, 2 * d:3 * d]).astype(bf16)


def _qkv_proj(h, g, w):
    n, d = h.shape
    row = pl.BlockSpec((ROW_TILE, d), lambda i: (i, 0))
    out = jax.ShapeDtypeStruct((n, d), bf16)
    return pl.pallas_call(
        _qkv_kernel,
        grid=(n // ROW_TILE,),
        in_specs=[row, _const_spec((1, d)), _const_spec((d, 3 * d))],
        out_specs=[row, row, row],
        out_shape=[out, out, out],
        compiler_params=_params(1),
        name="qkv_proj",
    )(h, g, w)


def _kvq_kernel(x_ref, gkv_ref, gq_ref, wkv_ref, wf_ref, bf_ref, wq_ref,
                k_ref, v_ref, q_ref, fc_ref, carry_ref, *, tiles_per_seq):
    d = D_MODEL
    x = x_ref[...]
    xn = x * _inv_rms(x)
    s_in = (xn * gkv_ref[...]).astype(bf16)
    a = (xn * gq_ref[...]).astype(bf16)
    k_ref[...] = _dot(s_in, wkv_ref[:, 0:d]).astype(bf16)
    v_ref[...] = _dot(s_in, wkv_ref[:, d:2 * d]).astype(bf16)
    q_ref[...] = (_dot(a, wq_ref[...]) * Q_SCALE).astype(bf16)

    fl = _dot(s_in, wf_ref[...]) + bf_ref[...]
    log_f = (jnp.minimum(fl, 0.0)
             - jnp.log(1.0 + jnp.exp(-jnp.abs(fl)))) * LOG2E
    hi = log_f.astype(bf16)
    lo = (log_f - hi.astype(f32)).astype(bf16)
    r = lax.broadcasted_iota(jnp.int32, (ROW_TILE, ROW_TILE), 0)
    c = lax.broadcasted_iota(jnp.int32, (ROW_TILE, ROW_TILE), 1)
    tri = (r >= c).astype(bf16)
    csum = _dot(tri, hi) + _dot(tri, lo)

    @pl.when(pl.program_id(0) % tiles_per_seq == 0)
    def _():
        carry_ref[...] = jnp.zeros_like(carry_ref)

    fcum = csum + carry_ref[0:1, :]
    fc_ref[...] = fcum
    carry_ref[...] = jnp.broadcast_to(fcum[ROW_TILE - 1:ROW_TILE, :],
                                      carry_ref.shape)


def _kvq_proj(h, gkv, gq, wkv, wf, bfp, wq, *, seq):
    n, d = h.shape
    row = pl.BlockSpec((ROW_TILE, d), lambda i: (i, 0))
    out = jax.ShapeDtypeStruct((n, d), bf16)
    return pl.pallas_call(
        functools.partial(_kvq_kernel, tiles_per_seq=seq // ROW_TILE),
        grid=(n // ROW_TILE,),
        in_specs=[row, _const_spec((1, d)), _const_spec((1, d)),
                  _const_spec((d, 2 * d)), _const_spec((d, LANES)),
                  _const_spec((1, LANES)), _const_spec((d, d))],
        out_specs=[row, row, row,
                   pl.BlockSpec((ROW_TILE, LANES), lambda i: (i, 0))],
        out_shape=[out, out, out, jax.ShapeDtypeStruct((n, LANES), f32)],
        scratch_shapes=[pltpu.VMEM((SUBLANES, LANES), f32)],
        compiler_params=_params(1),
        name="kvq_proj",
    )(h, gkv, gq, wkv, wf, bfp, wq)


def _split_q(q_ref, qs_ref):
    q2 = q_ref[0]
    lo = lax.broadcasted_iota(jnp.int32, q2.shape, 1) < HEAD_DIM
    zero = jnp.zeros_like(q2)
    qs_ref[0] = jnp.where(lo, q2, zero)
    qs_ref[1] = jnp.where(lo, zero, q2)


def _neg_abs(x):
    bits = lax.bitcast_convert_type(x, jnp.uint32) | jnp.uint32(0x80000000)
    return lax.bitcast_convert_type(bits, f32)


def _block_mask(rows, cols, offset, strict):
    r = lax.broadcasted_iota(jnp.int32, (rows, cols), 0) + offset
    c = lax.broadcasted_iota(jnp.int32, (rows, cols), 1)
    return c < r if strict else c <= r


def _tile_row(i):
    return pl.multiple_of(i * K_TILE, K_TILE)


def _next_tile(last, qs, kj_b, n_super):
    tiles = Q_SUPER // K_TILE
    nqs = jnp.where(last, jnp.minimum(qs + 1, n_super - 1), qs)
    nkj = jnp.where(last, nqs * tiles + tiles - 1, kj_b - 1)
    return pl.multiple_of(nqs * Q_SUPER, Q_SUPER), nkj


def _lo_half(rows):
    return lax.broadcasted_iota(jnp.int32, (rows, LANES), 1) < HEAD_DIM


def _lower_rows(x, fill):
    r = lax.broadcasted_iota(jnp.int32, (Q_SUPER, x.shape[1]), 0)
    return jnp.where(r >= K_TILE, jnp.concatenate([x, x], axis=0), fill)


def _put_pair(ref, vals):
    ref[0] = vals[0]
    ref[1] = vals[1]


def _sb_attn_kernel(q_ref, k_ref, v_ref, nu_ref, o_ref,
                    qs_ref, z_ref, w_ref, acc_ref, *, seq):
    n_super = seq // Q_SUPER
    tiles = Q_SUPER // K_TILE
    lane = lax.broadcasted_iota(jnp.int32, (Q_SUPER, LANES), 1)
    lo_half = lane < HEAD_DIM

    def qk(qrow, kj):
        kt = k_ref[0, pl.ds(_tile_row(kj), K_TILE), :]
        return tuple(_dot_nt(qs_ref[h, pl.ds(qrow, Q_SUPER), :], kt)
                     for h in range(2))

    def weights(z, survs, mask):
        ws, new_survs = [], []
        for h in range(2):
            sp = (jnp.maximum(z[h], 0.0)
                  + jnp.log(1.0 + jnp.exp2(_neg_abs(z[h]))) * LOG2E)
            if mask is not None:
                sp = jnp.where(mask, sp, 0.0)
            later = _dot(sp.astype(bf16), nu_ref[...])
            logw = (z[h] - sp) + later + survs[h]
            if mask is not None:
                logw = jnp.where(mask, logw, NEG_BIG)
            ws.append(jnp.exp2(logw).astype(bf16))
            new_survs.append(survs[h] - jnp.sum(sp, axis=-1, keepdims=True))
        return tuple(ws), tuple(new_survs)

    def pv(ws, kj):
        vt = v_ref[0, pl.ds(_tile_row(kj), K_TILE), :]
        return jnp.where(_lo_half(ws[0].shape[0]),
                         _dot(ws[0], vt), _dot(ws[1], vt))

    _split_q(q_ref, qs_ref)
    acc_ref[...] = jnp.zeros_like(acc_ref)
    w_ref[...] = jnp.zeros_like(w_ref)
    _put_pair(z_ref, qk(0, tiles - 1))

    def super_tile(qs, carry):
        q0 = pl.multiple_of(qs * Q_SUPER, Q_SUPER)
        kj_a = qs * tiles + 1
        kj_b = qs * tiles
        zb = qk(q0, kj_b)
        prev = pl.multiple_of(jnp.maximum(qs - 1, 0) * Q_SUPER, Q_SUPER)
        o_ref[0, pl.ds(prev, Q_SUPER), :] = (
            acc_ref[...] + pv((w_ref[0], w_ref[1]), 0)).astype(o_ref.dtype)
        zero = jnp.zeros((K_TILE, 1), f32)
        w_a, survs = weights(
            (z_ref[0, K_TILE:, :], z_ref[1, K_TILE:, :]), (zero, zero),
            _block_mask(K_TILE, K_TILE, 0, strict=True))
        survs = tuple(_lower_rows(s, 0.0) for s in survs)
        _put_pair(z_ref, qk(*_next_tile(qs == 0, qs, kj_b, n_super)))
        w_b, survs = weights(zb, survs,
                             _block_mask(Q_SUPER, K_TILE, 0, strict=True))
        acc_ref[:K_TILE, :] = jnp.zeros((K_TILE, LANES), f32)
        acc_ref[K_TILE:, :] = pv(w_a, kj_a)
        _put_pair(w_ref, w_b)

        def pair(j, survs):
            kj_a = qs * tiles - 1 - 2 * j
            kj_b = kj_a - 1
            zb = qk(q0, kj_b)
            w_a, survs = weights((z_ref[0], z_ref[1]), survs, None)
            acc_ref[...] += pv((w_ref[0], w_ref[1]), kj_a + 1)
            _put_pair(z_ref, qk(*_next_tile(j == qs - 1, qs, kj_b, n_super)))
            w_b, survs = weights(zb, survs, None)
            acc_ref[...] += pv(w_a, kj_a)
            _put_pair(w_ref, w_b)
            return survs

        lax.fori_loop(0, qs, pair, survs)
        return carry

    lax.fori_loop(0, n_super, super_tile, 0)
    o_ref[0, pl.ds(seq - Q_SUPER, Q_SUPER), :] = (
        acc_ref[...] + pv((w_ref[0], w_ref[1]), 0)).astype(o_ref.dtype)


def _sb_attention(q, k, v, neg_upper):
    b, s, d = q.shape
    blk = pl.BlockSpec((1, s, LANES), lambda bi, hp: (bi, 0, hp))
    return pl.pallas_call(
        functools.partial(_sb_attn_kernel, seq=s),
        grid=(b, d // LANES),
        in_specs=[blk, blk, blk, _const_spec((K_TILE, K_TILE))],
        out_specs=blk,
        out_shape=jax.ShapeDtypeStruct((b, s, d), bf16),
        scratch_shapes=[pltpu.VMEM((2, s, LANES), bf16),
                        pltpu.VMEM((2, Q_SUPER, K_TILE), f32),
                        pltpu.VMEM((2, Q_SUPER, K_TILE), bf16),
                        pltpu.VMEM((Q_SUPER, LANES), f32)],
        compiler_params=_params(2),
        name="sb_attention",
    )(q, k, v, neg_upper)


def _fox_attn_kernel(q_ref, k_ref, v_ref, fc_ref, fr_ref, swap_ref, o_ref,
                     qs_ref, z_ref, p_ref, num_ref, den_ref, *, seq):
    hp = pl.program_id(1)
    n_super = seq // Q_SUPER
    tiles = Q_SUPER // K_TILE
    lane = lax.broadcasted_iota(jnp.int32, (Q_SUPER, LANES), 1)
    lo_half = lane < HEAD_DIM
    klane = lax.broadcasted_iota(jnp.int32, (K_TILE, LANES), 1)
    own_half = (klane < HEAD_DIM, klane >= HEAD_DIM)

    def qk(qrow, kj):
        kt = k_ref[0, pl.ds(_tile_row(kj), K_TILE), :]
        return tuple(_dot_nt(qs_ref[h, pl.ds(qrow, Q_SUPER), :], kt)
                     for h in range(2))

    def probs(z, fq, kj, ms, masked):
        rows = z[0].shape[0]
        ps, alphas, new_ms = [], [], []
        for h in range(2):
            fk = fr_ref[0, 2 * hp + h, pl.ds(kj, 1), :]
            p_rows, a_rows, m_rows = [], [], []
            for r0 in range(0, rows, ROW_CHUNK):
                rs = slice(r0, r0 + ROW_CHUNK)
                parts = []
                for c in range(0, K_TILE, LANES):
                    s = z[h][rs, c:c + LANES] + fq[h][rs] - fk[:, c:c + LANES]
                    if masked:
                        s = jnp.where(
                            _block_mask(ROW_CHUNK, LANES, r0 - c, strict=False),
                            s, NEG_BIG)
                    parts.append(s)
                row_max = jnp.max(functools.reduce(jnp.maximum, parts),
                                  axis=-1, keepdims=True)
                m_new = jnp.maximum(ms[h][rs], row_max)
                p_rows.append(jnp.concatenate(
                    [jnp.exp2(s - m_new) for s in parts], axis=1).astype(bf16))
                a_rows.append(jnp.exp2(ms[h][rs] - m_new))
                m_rows.append(m_new)
            ps.append(jnp.concatenate(p_rows, axis=0))
            alphas.append(jnp.concatenate(a_rows, axis=0))
            new_ms.append(jnp.concatenate(m_rows, axis=0))
        return tuple(ps), tuple(alphas), tuple(new_ms)

    def pv(ps, kj):
        vt = v_ref[0, pl.ds(_tile_row(kj), K_TILE), :].astype(f32)
        pvs = [_dot(ps[h], jnp.where(own_half[h], vt, 1.0).astype(bf16))
               for h in range(2)]
        lo = _lo_half(ps[0].shape[0])
        return jnp.where(lo, pvs[0], pvs[1]), jnp.where(lo, pvs[1], pvs[0])

    def rescaled(alphas, pvn, pvd):
        return (jnp.where(lo_half, alphas[0], alphas[1]) * num_ref[...] + pvn,
                jnp.where(lo_half, alphas[1], alphas[0]) * den_ref[...] + pvd)

    def accumulate(alphas, pvn, pvd):
        num, den = rescaled(alphas, pvn, pvd)
        num_ref[...] = num
        den_ref[...] = den

    def finish(alphas, rows):
        num, den = rescaled(alphas, *pv((p_ref[0], p_ref[1]), 0))
        hi = den.astype(bf16)
        rest = den - hi.astype(f32)
        mid = rest.astype(bf16)
        lo = (rest - mid.astype(f32)).astype(bf16)
        swapped = _dot(jnp.concatenate([hi, mid, lo], axis=1), swap_ref[...])
        o_ref[0, pl.ds(rows, Q_SUPER), :] = (num / swapped).astype(o_ref.dtype)

    _split_q(q_ref, qs_ref)
    num_ref[...] = jnp.zeros_like(num_ref)
    den_ref[...] = jnp.ones_like(den_ref)
    p_ref[...] = jnp.zeros_like(p_ref)
    _put_pair(z_ref, qk(0, tiles - 1))
    one = jnp.ones((Q_SUPER, LANES), f32)

    def super_tile(qs, last_alphas):
        q0 = pl.multiple_of(qs * Q_SUPER, Q_SUPER)
        kj_a = qs * tiles + 1
        kj_b = qs * tiles
        zb = qk(q0, kj_b)
        finish(last_alphas,
               pl.multiple_of(jnp.maximum(qs - 1, 0) * Q_SUPER, Q_SUPER))
        fc = fc_ref[0, pl.ds(q0, Q_SUPER), :]
        fq = tuple(
            jnp.broadcast_to(
                jnp.sum(jnp.where(lane == 2 * hp + h, fc, 0.0), axis=-1,
                        keepdims=True), (Q_SUPER, LANES)) for h in range(2))
        m0 = jnp.full((K_TILE, LANES), NEG_BIG, f32)
        p_a, _, ms = probs((z_ref[0, K_TILE:, :], z_ref[1, K_TILE:, :]),
                           tuple(f[K_TILE:, :] for f in fq), kj_a, (m0, m0),
                           True)
        ms = tuple(_lower_rows(m, NEG_BIG) for m in ms)
        _put_pair(z_ref, qk(*_next_tile(qs == 0, qs, kj_b, n_super)))
        p_b, al_b, ms = probs(zb, fq, kj_b, ms, True)
        pvn, pvd = pv(p_a, kj_a)
        num_ref[:K_TILE, :] = jnp.zeros((K_TILE, LANES), f32)
        num_ref[K_TILE:, :] = pvn
        den_ref[:K_TILE, :] = jnp.zeros((K_TILE, LANES), f32)
        den_ref[K_TILE:, :] = pvd
        _put_pair(p_ref, p_b)

        def pair(j, carry):
            ms, al_prev = carry
            kj_a = qs * tiles - 1 - 2 * j
            kj_b = kj_a - 1
            zb = qk(q0, kj_b)
            p_a, al_a, ms = probs((z_ref[0], z_ref[1]), fq, kj_a, ms, False)
            accumulate(al_prev, *pv((p_ref[0], p_ref[1]), kj_a + 1))
            _put_pair(z_ref, qk(*_next_tile(j == qs - 1, qs, kj_b, n_super)))
            p_b, al_b, ms = probs(zb, fq, kj_b, ms, False)
            accumulate(al_a, *pv(p_a, kj_a))
            _put_pair(p_ref, p_b)
            return ms, al_b

        _, last_alphas = lax.fori_loop(0, qs, pair, (ms, al_b))
        return last_alphas

    last_alphas = lax.fori_loop(0, n_super, super_tile, (one, one))
    finish(last_alphas, seq - Q_SUPER)


def _fox_attention(q, k, v, fcum_c, fcum_r):
    b, s, d = q.shape
    blk = pl.BlockSpec((1, s, LANES), lambda bi, hp: (bi, 0, hp))
    nk = s // K_TILE
    ri = lax.broadcasted_iota(jnp.int32, (3 * LANES, LANES), 0)
    ci = lax.broadcasted_iota(jnp.int32, (3 * LANES, LANES), 1)
    swap = (ri % LANES == (ci + HEAD_DIM) % LANES).astype(bf16)
    return pl.pallas_call(
        functools.partial(_fox_attn_kernel, seq=s),
        grid=(b, d // LANES),
        in_specs=[blk, blk, blk,
                  pl.BlockSpec((1, s, LANES), lambda bi, hp: (bi, 0, 0)),
                  pl.BlockSpec((1, N_HEADS, nk, K_TILE),
                               lambda bi, hp: (bi, 0, 0, 0)),
                  _const_spec((3 * LANES, LANES))],
        out_specs=blk,
        out_shape=jax.ShapeDtypeStruct((b, s, d), bf16),
        scratch_shapes=[pltpu.VMEM((2, s, LANES), bf16),
                        pltpu.VMEM((2, Q_SUPER, K_TILE), f32),
                        pltpu.VMEM((2, Q_SUPER, K_TILE), bf16),
                        pltpu.VMEM((Q_SUPER, LANES), f32),
                        pltpu.VMEM((Q_SUPER, LANES), f32)],
        compiler_params=_params(2),
        name="fox_attention",
    )(q, k, v, fcum_c, fcum_r, swap)


def _gelu_tanh(x):
    c = 0.7978845608028654
    return 0.5 * x * (1.0 + jnp.tanh(c * (x + 0.044715 * (x * x * x))))


def _ffn_kernel(o_ref, h_ref, wo_ref, go_ref, g1_ref, wup_ref, cw_ref, cb_ref,
                wdn_ref, g2_ref, out_ref, act_ref, halo_ref, *, tiles_per_seq):
    y = _dot(o_ref[...], wo_ref[...])
    x = h_ref[...] + y * _inv_rms(y) * go_ref[...]
    a = (x * _inv_rms(x) * g1_ref[...]).astype(bf16)

    @pl.when(pl.program_id(0) % tiles_per_seq == 0)
    def _():
        halo_ref[...] = jnp.zeros_like(halo_ref)

    def conv(cols):
        hdn = _dot(a, wup_ref[:, cols])
        xx = jnp.concatenate([halo_ref[:, cols], hdn], axis=0)
        halo_ref[:, cols] = hdn[ROW_TILE - SUBLANES:, :]
        prev1 = pltpu.roll(xx, 1, 0)[SUBLANES:, :]
        prev2 = pltpu.roll(xx, 2, 0)[SUBLANES:, :]
        return (cw_ref[2:3, cols] * hdn + cw_ref[1:2, cols] * prev1
                + cw_ref[0:1, cols] * prev2 + cb_ref[:, cols])

    for c in range(D_FF // FF_CHUNK):
        gate = conv(slice(c * FF_CHUNK, (c + 1) * FF_CHUNK))
        up = conv(slice(D_FF + c * FF_CHUNK, D_FF + (c + 1) * FF_CHUNK))
        act_ref[:, c * FF_CHUNK:(c + 1) * FF_CHUNK] = (
            _gelu_tanh(gate) * up).astype(bf16)

    f = _dot(act_ref[...], wdn_ref[...])
    out_ref[...] = x + f * _inv_rms(f) * g2_ref[...]


def _attn_out_ffn(o, h, wo, go, g1, wup, cw, cb, wdn, g2, *, seq):
    n, d = h.shape
    row = pl.BlockSpec((ROW_TILE, d), lambda i: (i, 0))
    return pl.pallas_call(
        functools.partial(_ffn_kernel, tiles_per_seq=seq // ROW_TILE),
        grid=(n // ROW_TILE,),
        in_specs=[row, row, _const_spec((d, d)), _const_spec((1, d)),
                  _const_spec((1, d)), _const_spec((d, 2 * D_FF)),
                  _const_spec((3, 2 * D_FF)), _const_spec((1, 2 * D_FF)),
                  _const_spec((D_FF, d)), _const_spec((1, d))],
        out_specs=row,
        out_shape=jax.ShapeDtypeStruct((n, d), f32),
        scratch_shapes=[pltpu.VMEM((ROW_TILE, D_FF), bf16),
                        pltpu.VMEM((SUBLANES, 2 * D_FF), f32)],
        compiler_params=_params(1),
        name="attn_out_ffn",
    )(o, h, wo, go, g1, wup, cw, cb, wdn, g2)


def kernel(x, sb_pre_g, sb_w_qkv, sb_w_o, sb_post_g, kv_norm_g, w_kvf, b_f,
           fox_pre_g, fox_w_q, fox_w_o, fox_post_g,
           ffn_pre_g, w_up, conv_w, conv_b, w_down, ffn_post_g):
    b, s, d = x.shape
    n = b * s
    assert d == D_MODEL and s % ROW_TILE == 0 and s % Q_SUPER == 0
    assert sb_pre_g.shape[0] == 1 and fox_pre_g.shape[0] == 1

    def row(g):
        return g.reshape(1, -1).astype(f32)

    def out_ffn(o, h, wo, go, layer):
        return _attn_out_ffn(
            o.reshape(n, d), h, wo.astype(bf16), row(go),
            row(ffn_pre_g[layer]), w_up[layer].astype(bf16),
            conv_w[layer].astype(f32), row(conv_b[layer]),
            w_down[layer].astype(bf16), row(ffn_post_g[layer]), seq=s)

    h = x.reshape(n, d)

    q, k, v = _qkv_proj(h, row(sb_pre_g[0]), sb_w_qkv[0].astype(bf16))
    ri = lax.broadcasted_iota(jnp.int32, (K_TILE, K_TILE), 0)
    ci = lax.broadcasted_iota(jnp.int32, (K_TILE, K_TILE), 1)
    neg_upper = jnp.where(ri > ci, -1.0, 0.0).astype(bf16)
    o = _sb_attention(q.reshape(b, s, d), k.reshape(b, s, d),
                      v.reshape(b, s, d), neg_upper)
    h = out_ffn(o, h, sb_w_o[0], sb_post_g[0], 0)

    w_f = jnp.zeros((d, LANES), bf16).at[:, :N_HEADS].set(
        w_kvf[:, 2 * d:].astype(bf16))
    b_fp = jnp.zeros((1, LANES), f32).at[0, :N_HEADS].set(b_f.astype(f32))
    k, v, q, fcum = _kvq_proj(h, row(kv_norm_g), row(fox_pre_g[0]),
                              w_kvf[:, :2 * d].astype(bf16), w_f, b_fp,
                              fox_w_q[0].astype(bf16), seq=s)
    fcum_c = fcum.reshape(b, s, LANES)
    fcum_r = jnp.transpose(fcum_c[:, :, :N_HEADS], (0, 2, 1)).reshape(
        b, N_HEADS, s // K_TILE, K_TILE)
    o = _fox_attention(q.reshape(b, s, d), k.reshape(b, s, d),
                       v.reshape(b, s, d), fcum_c, fcum_r)
    h = out_ffn(o, h, fox_w_o[0], fox_post_g[0], 1)
    return h.reshape(b, s, d)
```
